```python
import math
import jax
import jax.numpy as jnp
from jax import lax
import numpy as np

D_MODEL = 4096
BATCH = 4
SEQ = 2048
DEPTH = 4
DEC_BATCH = 8
DEC_SEQ = 4
PAST_LEN = 8192
PAGE_SIZE = 128

HEAD_DIM = 128
N_A_LAYERS = DEPTH // 2
N_B_LAYERS = DEPTH - N_A_LAYERS
N_DENSE_LAYERS = (DEPTH + 1) // 2
N_MOE_LAYERS = DEPTH // 2
A_Q_HEADS = D_MODEL // HEAD_DIM
A_KV_HEADS = 8
IDX_HEADS = 32
IDX_DIM = 128
IDX_TOPK_MAX = 256
A_Q_COLS = A_Q_HEADS * HEAD_DIM
A_KV_COLS = A_KV_HEADS * HEAD_DIM
A_IDXQ_COLS = IDX_HEADS * IDX_DIM
A_SPLITS = (A_Q_COLS, A_Q_COLS + A_KV_COLS, A_Q_COLS + 2 * A_KV_COLS,
            A_Q_COLS + 2 * A_KV_COLS + A_IDXQ_COLS,
            A_Q_COLS + 2 * A_KV_COLS + A_IDXQ_COLS + IDX_DIM)
A_IN_COLS = A_SPLITS[-1] + IDX_HEADS
B_Q_HEADS = D_MODEL // HEAD_DIM
B_KV_HEADS = 8
DILATED_GROUPS = ((128, 1), (512, 4), (2048, 16))
N_GROUPS = len(DILATED_GROUPS)
W_MAX = max(w for w, _ in DILATED_GROUPS)
B_Q_COLS = N_GROUPS * B_Q_HEADS * HEAD_DIM
B_KV_COLS = B_KV_HEADS * HEAD_DIM
D_FF = 14336
N_EXPERTS = 8
TOP_K_EXPERTS = 2
D_FF_EXPERT = 14336
ROPE_THETA = 10000.0
Q_BLOCK = 128
LN_EPS = 1e-5
DEEPNORM_ALPHA = (2.0 * DEPTH) ** 0.25
DEEPNORM_BETA = (8.0 * DEPTH) ** -0.25
ATTN_SCALE = HEAD_DIM ** -0.5
IDX_SCALE = (IDX_HEADS ** -0.5) * (IDX_DIM ** -0.5)

kernel_name = 'yoco_dsa_dilated_moe_step'


def rope(x, pos):
    half = x.shape[-1] // 2
    inv_freq = jnp.power(ROPE_THETA, -jnp.arange(half, dtype=jnp.float32) / half)
    ang = pos.astype(jnp.float32)[:, None] * inv_freq[None, :]
    cos = jnp.cos(ang)[:, None, :]
    sin = jnp.sin(ang)[:, None, :]
    x32 = x.astype(jnp.float32)
    x1, x2 = x32[..., :half], x32[..., half:]
    return jnp.concatenate([x1 * cos - x2 * sin, x2 * cos + x1 * sin], axis=-1).astype(x.dtype)


def layer_norm(x, g, b):
    x32 = x.astype(jnp.float32)
    mu = jnp.mean(x32, axis=-1, keepdims=True)
    var = jnp.mean(jnp.square(x32 - mu), axis=-1, keepdims=True)
    y = (x32 - mu) * lax.rsqrt(var + LN_EPS) * g.astype(jnp.float32) + b.astype(jnp.float32)
    return y.astype(x.dtype)


def deepnorm(x, sub, g, b):
    return layer_norm(DEEPNORM_ALPHA * x + sub, g, b)


def blocked_queries(fn, q_inputs, q_pos):
    n_blk = q_pos.shape[0] // Q_BLOCK

    def to_blocks(a):
        return jnp.moveaxis(a.reshape(a.shape[0], n_blk, Q_BLOCK, *a.shape[2:]), 1, 0)

    xs = tuple(to_blocks(a) for a in q_inputs) + (q_pos.reshape(n_blk, Q_BLOCK),)
    out = lax.map(lambda blk: fn(*blk), xs)
    out = jnp.moveaxis(out, 0, 1)
    return out.reshape(out.shape[0], n_blk * Q_BLOCK, *out.shape[3:])


def a_project(h, w_in, pos):
    b, t = h.shape[0], h.shape[1]
    q, k, v, iq, ik, iw = jnp.split(h @ w_in, list(A_SPLITS), axis=-1)
    q = rope(q.reshape(b, t, A_Q_HEADS, HEAD_DIM), pos)
    k = rope(k.reshape(b, t, A_KV_HEADS, HEAD_DIM), pos)
    v = v.reshape(b, t, A_KV_HEADS, HEAD_DIM)
    iq = rope(iq.reshape(b, t, IDX_HEADS, IDX_DIM), pos)
    ik = rope(ik[:, :, None, :], pos)[:, :, 0, :]
    return q, k, v, iq, ik, iw


def dsa_attend(q, iq, iw, k, v, ik, q_pos, n_sel):
    b, t = q.shape[0], q.shape[1]
    n_keys = k.shape[1]
    rel = jnp.einsum('bthd,bsd->bths', iq, ik, preferred_element_type=jnp.float32)
    score = jnp.einsum('bth,bths->bts', iw.astype(jnp.float32), jax.nn.relu(rel)) * IDX_SCALE
    admissible = jnp.arange(n_keys, dtype=jnp.int32)[None, :] <= q_pos[:, None]
    score = jnp.where(admissible[None], score, -jnp.inf)
    _, sel = lax.top_k(score, n_sel)
    valid = sel <= q_pos[None, :, None]
    take = jax.vmap(lambda rows, idx: rows[idx])
    k_sel = take(k, sel)
    v_sel = take(v, sel)
    qg = q.reshape(b, t, A_KV_HEADS, A_Q_HEADS // A_KV_HEADS, HEAD_DIM)
    s = jnp.einsum('bthgd,btkhd->bthgk', qg, k_sel, preferred_element_type=jnp.float32) * ATTN_SCALE
    s = jnp.where(valid[:, :, None, None, :], s, -jnp.inf)
    p = jax.nn.softmax(s, axis=-1).astype(v.dtype)
    o = jnp.einsum('bthgk,btkhd->bthgd', p, v_sel)
    return o.reshape(b, t, A_Q_COLS)


def shared_kv(h, w_kv, pos):
    b, t = h.shape[0], h.shape[1]
    k, v = jnp.split(h @ w_kv, [B_KV_COLS], axis=-1)
    k = rope(k.reshape(b, t, B_KV_HEADS, HEAD_DIM), pos)
    return k, v.reshape(b, t, B_KV_HEADS, HEAD_DIM)


def b_queries(h, w_q, pos):
    b, t = h.shape[0], h.shape[1]
    q = rope((h @ w_q).reshape(b, t, N_GROUPS * B_Q_HEADS, HEAD_DIM), pos)
    return q.reshape(b, t, N_GROUPS, B_Q_HEADS, HEAD_DIM)


def pad_front(a):
    return jnp.pad(a, ((0, 0), (W_MAX, 0), (0, 0), (0, 0)))


def dilated_attend(q, k_pad, v_pad, q_pos, idx_offset):
    b, t = q.shape[0], q.shape[1]
    rep = B_Q_HEADS // B_KV_HEADS
    outs, lses = [], []
    for g, (window, dilation) in enumerate(DILATED_GROUPS):
        n_taps = window // dilation + 1
        key_pos = q_pos[:, None] - dilation * jnp.arange(n_taps, dtype=jnp.int32)[None, :]
        valid = key_pos >= 0
        idx = key_pos + idx_offset
        k_g = k_pad[:, idx]
        v_g = v_pad[:, idx]
        qg = q[:, :, g].reshape(b, t, B_KV_HEADS, rep, HEAD_DIM)
        s = jnp.einsum('bthrd,btnhd->bthrn', qg, k_g, preferred_element_type=jnp.float32) * ATTN_SCALE
        s = jnp.where(valid[None, :, None, None, :], s, -jnp.inf)
        lse = jax.nn.logsumexp(s, axis=-1)
        p = jnp.exp(s - lse[..., None]).astype(v_pad.dtype)
        outs.append(jnp.einsum('bthrn,btnhd->bthrd', p, v_g))
        lses.append(lse)
    mix = jax.nn.softmax(jnp.stack(lses, axis=0), axis=0)
    o = mix[0][..., None] * outs[0].astype(jnp.float32)
    for g in range(1, N_GROUPS):
        o = o + mix[g][..., None] * outs[g].astype(jnp.float32)
    return o.reshape(b, t, B_Q_HEADS * HEAD_DIM).astype(q.dtype)


def swiglu(x, w_in, w_down):
    gate, up = jnp.split(x @ w_in, 2, axis=-1)
    return (jax.nn.silu(gate) * up) @ w_down


def moe_ffn(h, w_router, w_in, w_down):
    x2 = h.reshape(-1, h.shape[-1])
    logits = jnp.dot(x2, w_router, preferred_element_type=jnp.float32)
    top_val, top_idx = lax.top_k(logits, TOP_K_EXPERTS)
    gates = jax.nn.softmax(top_val, axis=-1)
    dense_gate = jnp.sum(jax.nn.one_hot(top_idx, N_EXPERTS, dtype=jnp.float32) * gates[..., None], axis=-2)
    y = jnp.zeros(x2.shape, jnp.float32)
    for e in range(N_EXPERTS):
        y = y + dense_gate[:, e:e + 1] * swiglu(x2, w_in[e], w_down[e]).astype(jnp.float32)
    return y.reshape(h.shape).astype(h.dtype)


def gather_pages(pool, layer, page_table):
    pages = pool[layer, page_table]
    return pages.reshape(pages.shape[0], pages.shape[1] * pages.shape[2], *pages.shape[3:])


def _uniform(key, shape, std):
    a = std * math.sqrt(3.0)
    return jax.random.uniform(key, shape, jnp.float32, -a, a)


def setup_inputs(seed: int = 0) -> dict:
    key = jax.random.key(seed)
    ks = jax.random.split(key, 24)
    n_pages = PAST_LEN // PAGE_SIZE
    n_used = DEC_BATCH * n_pages
    n_pool = n_used + max(1, n_used // 4)
    w_buf = min(W_MAX, PAST_LEN)
    x_prompt = jax.random.normal(ks[0], (BATCH, SEQ, D_MODEL), jnp.float32)
    x_sample = jax.random.normal(ks[1], (DEC_BATCH, DEC_SEQ, D_MODEL), jnp.float32)
    cache_a_k = jax.random.normal(ks[2], (N_A_LAYERS, n_pool, PAGE_SIZE, A_KV_HEADS, HEAD_DIM), jnp.float32)
    cache_a_v = jax.random.normal(ks[3], (N_A_LAYERS, n_pool, PAGE_SIZE, A_KV_HEADS, HEAD_DIM), jnp.float32)
    cache_a_idx = jax.random.normal(ks[4], (N_A_LAYERS, n_pool, PAGE_SIZE, IDX_DIM), jnp.float32)
    state_b_k = jax.random.normal(ks[5], (DEC_BATCH, w_buf, B_KV_HEADS, HEAD_DIM), jnp.float32)
    state_b_v = jax.random.normal(ks[6], (DEC_BATCH, w_buf, B_KV_HEADS, HEAD_DIM), jnp.float32)
    page_table = jax.random.permutation(ks[7], n_pool)[:n_used].reshape(DEC_BATCH, n_pages).astype(jnp.int32)
    w_a_in = _uniform(ks[8], (N_A_LAYERS, D_MODEL, A_IN_COLS), D_MODEL ** -0.5)
    w_a_out = _uniform(ks[9], (N_A_LAYERS, A_Q_COLS, D_MODEL), DEEPNORM_BETA * A_Q_COLS ** -0.5)
    w_kv_shared = _uniform(ks[10], (D_MODEL, 2 * B_KV_COLS), D_MODEL ** -0.5)
    w_b_q = _uniform(ks[11], (N_B_LAYERS, D_MODEL, B_Q_COLS), D_MODEL ** -0.5)
    w_b_out = _uniform(ks[12], (N_B_LAYERS, B_Q_HEADS * HEAD_DIM, D_MODEL), DEEPNORM_BETA * (B_Q_HEADS * HEAD_DIM) ** -0.5)
    ln_mix_g = 1.0 + 0.02 * jax.random.normal(ks[13], (DEPTH, D_MODEL), jnp.float32)
    ln_mix_b = 0.02 * jax.random.normal(ks[14], (DEPTH, D_MODEL), jnp.float32)
    ln_ffn_g = 1.0 + 0.02 * jax.random.normal(ks[15], (DEPTH, D_MODEL), jnp.float32)
    ln_ffn_b = 0.02 * jax.random.normal(ks[16], (DEPTH, D_MODEL), jnp.float32)
    w_ffn_in = _uniform(ks[17], (N_DENSE_LAYERS, D_MODEL, 2 * D_FF), D_MODEL ** -0.5)
    w_ffn_down = _uniform(ks[18], (N_DENSE_LAYERS, D_FF, D_MODEL), DEEPNORM_BETA * D_FF ** -0.5)
    w_router = _uniform(ks[19], (N_MOE_LAYERS, D_MODEL, N_EXPERTS), D_MODEL ** -0.5)
    w_exp_in = _uniform(ks[20], (N_MOE_LAYERS, N_EXPERTS, D_MODEL, 2 * D_FF_EXPERT), D_MODEL ** -0.5)
    w_exp_down = _uniform(ks[21], (N_MOE_LAYERS, N_EXPERTS, D_FF_EXPERT, D_MODEL), DEEPNORM_BETA * D_FF_EXPERT ** -0.5)
    return {'x_prompt': x_prompt, 'x_sample': x_sample,
            'cache_a_k': cache_a_k, 'cache_a_v': cache_a_v, 'cache_a_idx': cache_a_idx,
            'state_b_k': state_b_k, 'state_b_v': state_b_v, 'page_table': page_table,
            'w_a_in': w_a_in, 'w_a_out': w_a_out, 'w_kv_shared': w_kv_shared,
            'w_b_q': w_b_q, 'w_b_out': w_b_out,
            'ln_mix_g': ln_mix_g, 'ln_mix_b': ln_mix_b, 'ln_ffn_g': ln_ffn_g, 'ln_ffn_b': ln_ffn_b,
            'w_ffn_in': w_ffn_in, 'w_ffn_down': w_ffn_down,
            'w_router': w_router, 'w_exp_in': w_exp_in, 'w_exp_down': w_exp_down}


def reference(x_prompt, x_sample, cache_a_k, cache_a_v, cache_a_idx, state_b_k, state_b_v, page_table,
              w_a_in, w_a_out, w_kv_shared, w_b_q, w_b_out, ln_mix_g, ln_mix_b, ln_ffn_g, ln_ffn_b,
              w_ffn_in, w_ffn_down, w_router, w_exp_in, w_exp_down):
    pos_p = jnp.arange(SEQ, dtype=jnp.int32)
    pos_s = PAST_LEN + jnp.arange(DEC_SEQ, dtype=jnp.int32)
    n_sel_p = min(IDX_TOPK_MAX, SEQ // 4)
    n_sel_s = min(IDX_TOPK_MAX, (PAST_LEN + DEC_SEQ) // 4)
    w_buf = min(W_MAX, PAST_LEN)
    hp, hs = x_prompt, x_sample
    ak_p, av_p, ai_p, ak_s, av_s, ai_s = [], [], [], [], [], []
    for l in range(DEPTH):
        if l < N_A_LAYERS:
            q, k, v, iq, ik, iw = a_project(hp, w_a_in[l], pos_p)
            mix_p = blocked_queries(
                lambda qb, iqb, iwb, pb: dsa_attend(qb, iqb, iwb, k, v, ik, pb, n_sel_p), (q, iq, iw), pos_p)
            ak_p.append(k)
            av_p.append(v)
            ai_p.append(ik)
            qs, ks_new, vs_new, iqs, iks_new, iws = a_project(hs, w_a_in[l], pos_s)
            k_all = jnp.concatenate([gather_pages(cache_a_k, l, page_table), ks_new], axis=1)
            v_all = jnp.concatenate([gather_pages(cache_a_v, l, page_table), vs_new], axis=1)
            ik_all = jnp.concatenate([gather_pages(cache_a_idx, l, page_table), iks_new], axis=1)
            mix_s = dsa_attend(qs, iqs, iws, k_all, v_all, ik_all, pos_s, n_sel_s)
            ak_s.append(ks_new)
            av_s.append(vs_new)
            ai_s.append(iks_new)
            hp = deepnorm(hp, mix_p @ w_a_out[l], ln_mix_g[l], ln_mix_b[l])
            hs = deepnorm(hs, mix_s @ w_a_out[l], ln_mix_g[l], ln_mix_b[l])
        else:
            j = l - N_A_LAYERS
            qp = b_queries(hp, w_b_q[j], pos_p)
            mix_p = blocked_queries(
                lambda qb, pb: dilated_attend(qb, kb_pad_p, vb_pad_p, pb, W_MAX), (qp,), pos_p)
            qs = b_queries(hs, w_b_q[j], pos_s)
            mix_s = dilated_attend(qs, kb_pad_s, vb_pad_s, pos_s, W_MAX - (PAST_LEN - w_buf))
            hp = deepnorm(hp, mix_p @ w_b_out[j], ln_mix_g[l], ln_mix_b[l])
            hs = deepnorm(hs, mix_s @ w_b_out[j], ln_mix_g[l], ln_mix_b[l])
        f = l // 2
        if l % 2 == 0:
            ffn_p = swiglu(hp, w_ffn_in[f], w_ffn_down[f])
            ffn_s = swiglu(hs, w_ffn_in[f], w_ffn_down[f])
        else:
            ffn_p = moe_ffn(hp, w_router[f], w_exp_in[f], w_exp_down[f])
            ffn_s = moe_ffn(hs, w_router[f], w_exp_in[f], w_exp_down[f])
        hp = deepnorm(hp, ffn_p, ln_ffn_g[l], ln_ffn_b[l])
        hs = deepnorm(hs, ffn_s, ln_ffn_g[l], ln_ffn_b[l])
        if l == N_A_LAYERS - 1:
            kb_p, vb_p = shared_kv(hp, w_kv_shared, pos_p)
            kb_new, vb_new = shared_kv(hs, w_kv_shared, pos_s)
            ctx_k = jnp.concatenate([state_b_k, kb_new], axis=1)
            ctx_v = jnp.concatenate([state_b_v, vb_new], axis=1)
            kb_pad_p, vb_pad_p = pad_front(kb_p), pad_front(vb_p)
            kb_pad_s, vb_pad_s = pad_front(ctx_k), pad_front(ctx_v)
            new_b_k_prompt = kb_p[:, SEQ - min(W_MAX, SEQ):]
            new_b_v_prompt = vb_p[:, SEQ - min(W_MAX, SEQ):]
            new_b_k_sample = ctx_k[:, DEC_SEQ:]
            new_b_v_sample = ctx_v[:, DEC_SEQ:]
    new_a_k_prompt = jnp.stack(ak_p, axis=0)
    new_a_v_prompt = jnp.stack(av_p, axis=0)
    new_a_idx_prompt = jnp.stack(ai_p, axis=0)
    new_a_k_sample = jnp.stack(ak_s, axis=0)
    new_a_v_sample = jnp.stack(av_s, axis=0)
    new_a_idx_sample = jnp.stack(ai_s, axis=0)
    return (hp, hs, new_a_k_prompt, new_a_v_prompt, new_a_idx_prompt, new_a_k_sample, new_a_v_sample,
            new_a_idx_sample, new_b_k_prompt, new_b_v_prompt, new_b_k_sample, new_b_v_sample)
```

```python
import functools
import math

import jax
import jax.numpy as jnp
from jax import lax
from jax.experimental import pallas as pl
from jax.experimental.pallas import tpu as pltpu

F32 = jnp.float32
BF16 = jnp.bfloat16
I32 = jnp.int32

LANES = 128
SUBLANES = 8
VMEM_LIMIT_BYTES = 56 * 1024 * 1024

D_MODEL = 4096
HEAD_DIM = 128
PAGE_SIZE = 128
A_Q_HEADS = 32
A_KV_HEADS = 8
IDX_HEADS = 32
IDX_TOPK_MAX = 256
B_Q_HEADS = 32
B_KV_HEADS = 8
DILATED_GROUPS = ((128, 1), (512, 4), (2048, 16))
N_EXPERTS = 8
D_FF = 14336
ROPE_THETA = 10000.0
Q_BLOCK = 128
LN_EPS = 1e-5
DEPTH = 4
DEEPNORM_ALPHA = (2.0 * DEPTH) ** 0.25
ATTN_SCALE = HEAD_DIM ** -0.5
IDX_SCALE = (IDX_HEADS ** -0.5) * (HEAD_DIM ** -0.5)
MASK_BIAS = -1e30
INT_MIN = -(2 ** 31)

TM_DENSE = 832
TM_MOE = 1024
TM_ROW = 128
TN_PROJ = 512
TN_FFN = 256


def _cparams(sem):
    return pltpu.CompilerParams(dimension_semantics=sem, vmem_limit_bytes=VMEM_LIMIT_BYTES)


def _rope_tile(y, cos, sin_signed):
    outs = []
    for g in range(y.shape[1] // HEAD_DIM):
        yh = y[:, g * HEAD_DIM:(g + 1) * HEAD_DIM]
        outs.append(yh * cos + pltpu.roll(yh, HEAD_DIM // 2, 1) * sin_signed)
    return outs[0] if len(outs) == 1 else jnp.concatenate(outs, axis=1)


def _mm_kernel(meta_ref, gid_ref, nv_ref, *refs, mode, nk):
    del meta_ref, gid_ref
    i = pl.program_id(0)
    k = pl.program_id(2)
    live = nv_ref[i] > 0
    if mode == "swiglu":
        x_ref, wg_ref, wu_ref, o_ref = refs

        @pl.when(live)
        def _():
            x = x_ref[...]
            g = jnp.dot(x, wg_ref[...].astype(BF16), preferred_element_type=F32)
            u = jnp.dot(x, wu_ref[...].astype(BF16), preferred_element_type=F32)
            o_ref[...] = (g * (1.0 / (1.0 + jnp.exp(-g))) * u).astype(o_ref.dtype)
    elif mode == "rope":
        x_ref, w_ref, cos_ref, sin_ref, o_ref = refs

        @pl.when(live)
        def _():
            y = jnp.dot(x_ref[...], w_ref[...].astype(BF16), preferred_element_type=F32)
            o_ref[...] = _rope_tile(y, cos_ref[...], sin_ref[...]).astype(o_ref.dtype)
    else:
        x_ref, w_ref, o_ref = refs

        @pl.when(live & (k == 0))
        def _():
            o_ref[...] = jnp.dot(x_ref[...], w_ref[...].astype(BF16),
                                 preferred_element_type=F32).astype(o_ref.dtype)

        if nk > 1:
            @pl.when(live & (k > 0))
            def _():
                o_ref[...] += jnp.dot(x_ref[...], w_ref[...].astype(BF16),
                                      preferred_element_type=F32)

    @pl.when(jnp.logical_not(live))
    def _():
        o_ref[...] = jnp.zeros(o_ref.shape, o_ref.dtype)


def _matmul(x, w, gid, nv, n_used, *, col0, n_out, tm, tn, tk=None, mode="plain",
            tabs=None, out_dtype=F32, up_col0=None, name="mm"):
    m, kdim = x.shape
    tk = kdim if tk is None else tk
    nk = kdim // tk
    assert m % tm == 0 and kdim % tk == 0 and col0 % tn == 0
    assert mode == "plain" or nk == 1
    assert out_dtype == F32 or nk == 1
    cb0 = col0 // tn
    grid = (m // tm, pl.cdiv(n_out, tn), nk)

    def x_map(i, j, k, meta, gid_r, nv_r):
        return (jnp.minimum(i, meta[0] - 1), k)

    def w_map_at(cb):
        def w_map(i, j, k, meta, gid_r, nv_r):
            live = nv_r[i] > 0
            return (gid_r[i], jnp.where(live, k, 0), cb + jnp.where(live, j, 0))
        return w_map

    def row_map(i, j, k, meta, gid_r, nv_r):
        return (i, 0)

    def o_map(i, j, k, meta, gid_r, nv_r):
        return (i, j)

    in_specs = [pl.BlockSpec((tm, tk), x_map), pl.BlockSpec((None, tk, tn), w_map_at(cb0))]
    args = [x, w]
    if mode == "swiglu":
        assert up_col0 % tn == 0
        in_specs.append(pl.BlockSpec((None, tk, tn), w_map_at(up_col0 // tn)))
        args.append(w)
    if mode == "rope":
        in_specs += [pl.BlockSpec((tm, HEAD_DIM), row_map)] * 2
        args += list(tabs)
    meta = jnp.reshape(n_used, (1,)).astype(I32)
    return pl.pallas_call(
        functools.partial(_mm_kernel, mode=mode, nk=nk),
        grid_spec=pltpu.PrefetchScalarGridSpec(
            num_scalar_prefetch=3, grid=grid, in_specs=in_specs,
            out_specs=pl.BlockSpec((tm, tn), o_map)),
        out_shape=jax.ShapeDtypeStruct((m, n_out), out_dtype),
        compiler_params=_cparams(("arbitrary", "arbitrary", "arbitrary")),
        name=name,
    )(meta, gid, nv, *args)


def _dense(x, w, layer, **kw):
    nt = x.shape[0] // kw["tm"]
    gid = jnp.full((nt,), layer, I32)
    nv = jnp.full((nt,), kw["tm"], I32)
    return _matmul(x, w, gid, nv, jnp.asarray(nt, I32), **kw)


def _layer_norm_rows(y, g, b):
    mu = jnp.mean(y, axis=-1, keepdims=True)
    d = y - mu
    var = jnp.mean(d * d, axis=-1, keepdims=True)
    return d * lax.rsqrt(var + LN_EPS) * g + b


def _deepnorm_kernel(h_ref, s_ref, g_ref, b_ref, o_ref, ob_ref):
    y = DEEPNORM_ALPHA * h_ref[...] + s_ref[...]
    o = _layer_norm_rows(y, g_ref[...], b_ref[...])
    o_ref[...] = o
    ob_ref[...] = o.astype(BF16)


def _deepnorm(h, sub, g, b):
    m, d = h.shape
    row = pl.BlockSpec((TM_ROW, d), lambda i: (i, 0))
    vec = pl.BlockSpec((1, d), lambda i: (0, 0))
    return pl.pallas_call(
        _deepnorm_kernel, grid=(m // TM_ROW,),
        in_specs=[row, row, vec, vec], out_specs=[row, row],
        out_shape=[jax.ShapeDtypeStruct((m, d), F32), jax.ShapeDtypeStruct((m, d), BF16)],
        compiler_params=_cparams(("arbitrary",)), name="deepnorm",
    )(h, sub, g.reshape(1, d), b.reshape(1, d))


def _row_count(mask):
    return jnp.sum(jnp.where(mask, 1.0, 0.0), axis=-1, keepdims=True)


def _select_bias(score, q_pos, n_sel):
    r_rows, n_keys = score.shape
    idx = lax.broadcasted_iota(I32, (r_rows, n_keys), 1)
    adm = idx <= q_pos
    sc = jnp.where(adm, score + 0.0, -jnp.inf)
    bits = pltpu.bitcast(sc, I32)
    key = bits ^ ((bits >> 31) & 0x7FFFFFFF)
    kf = float(n_sel)

    t0 = jnp.where(_row_count(key >= 0) >= kf, 0, INT_MIN).astype(I32)

    def value_step(i, t):
        cand = t | lax.shift_left(jnp.int32(1), jnp.int32(30) - i)
        return jnp.where(_row_count(key >= cand) >= kf, cand, t)

    thr = lax.fori_loop(0, 31, value_step, t0)
    gt = key > thr
    need = kf - _row_count(gt)
    tie_idx = jnp.where(key == thr, idx, jnp.int32(2 ** 30))
    n_bits = n_keys.bit_length()

    def index_step(i, j0):
        cand = j0 + lax.shift_left(jnp.int32(1), jnp.int32(n_bits - 1) - i)
        return jnp.where(_row_count(tie_idx < cand) < need, cand, j0)

    j0 = lax.fori_loop(0, n_bits, index_step, jnp.zeros((r_rows, 1), I32))
    picked = jnp.where(gt, 0.0, jnp.where(tie_idx <= j0, 0.0, MASK_BIAS))
    return jnp.where(adm, picked, MASK_BIAS)


def _a_select_prompt_kernel(iq_ref, ik_ref, iwt_ref, bias_ref, *, n_sel):
    qb = pl.program_id(1)
    ik = ik_ref[...].astype(BF16)
    acc = jnp.zeros((ik.shape[0], Q_BLOCK), F32)
    for h in range(IDX_HEADS):
        iqh = iq_ref[:, h * HEAD_DIM:(h + 1) * HEAD_DIM].astype(BF16)
        rel = lax.dot_general(ik, iqh, (((1,), (1,)), ((), ())), preferred_element_type=F32)
        acc = acc + jnp.maximum(rel, 0.0) * iwt_ref[h:h + 1, :]
    score = (acc * IDX_SCALE).T
    q_pos = qb * Q_BLOCK + lax.broadcasted_iota(I32, (Q_BLOCK, 1), 0)
    bias_ref[...] = _select_bias(score, q_pos, n_sel).astype(bias_ref.dtype)


def _a_select_prompt(iq, ik, iwt, *, batch, seq, n_sel):
    nqb = seq // Q_BLOCK
    return pl.pallas_call(
        functools.partial(_a_select_prompt_kernel, n_sel=n_sel),
        grid=(batch, nqb),
        in_specs=[pl.BlockSpec((Q_BLOCK, IDX_HEADS * HEAD_DIM), lambda b, q: (b * nqb + q, 0)),
                  pl.BlockSpec((seq, HEAD_DIM), lambda b, q: (b, 0)),
                  pl.BlockSpec((IDX_HEADS, Q_BLOCK), lambda b, q: (0, b * nqb + q))],
        out_specs=pl.BlockSpec((Q_BLOCK, seq), lambda b, q: (b * nqb + q, 0)),
        out_shape=jax.ShapeDtypeStruct((batch * seq, seq), BF16),
        compiler_params=_cparams(("arbitrary", "arbitrary")), name="a_select_prompt",
    )(iq, ik, iwt)


def _stack_heads(q_ref, rep):
    return jnp.concatenate(
        [q_ref[:, r * HEAD_DIM:(r + 1) * HEAD_DIM] for r in range(rep)], axis=0).astype(BF16)


def _a_attn_prompt_kernel(q_ref, k_ref, v_ref, bias_ref, o_ref, *, rep):
    k = k_ref[...].astype(BF16)
    v = v_ref[...].astype(BF16)
    bias = bias_ref[...].astype(F32)
    rows = q_ref.shape[0]
    s = lax.dot_general(_stack_heads(q_ref, rep), k, (((1,), (1,)), ((), ())),
                        preferred_element_type=F32) * ATTN_SCALE
    ps, ls = [], []
    for r in range(rep):
        sr = s[r * rows:(r + 1) * rows] + bias
        p = jnp.exp(sr - jnp.max(sr, axis=-1, keepdims=True))
        ls.append(jnp.sum(p, axis=-1, keepdims=True))
        ps.append(p.astype(BF16))
    o = jnp.dot(jnp.concatenate(ps, axis=0), v, preferred_element_type=F32)
    o = o * (1.0 / jnp.concatenate(ls, axis=0))
    for r in range(rep):
        o_ref[:, r * HEAD_DIM:(r + 1) * HEAD_DIM] = o[r * rows:(r + 1) * rows].astype(o_ref.dtype)


def _a_attn_prompt(q, k, v, bias, *, batch, seq):
    nqb = seq // Q_BLOCK
    rep = A_Q_HEADS // A_KV_HEADS
    qspec = pl.BlockSpec((Q_BLOCK, rep * HEAD_DIM), lambda b, h, q: (b * nqb + q, h))
    kvspec = pl.BlockSpec((seq, HEAD_DIM), lambda b, h, q: (b, h))
    return pl.pallas_call(
        functools.partial(_a_attn_prompt_kernel, rep=rep),
        grid=(batch, A_KV_HEADS, nqb),
        in_specs=[qspec, kvspec, kvspec,
                  pl.BlockSpec((Q_BLOCK, seq), lambda b, h, q: (b * nqb + q, 0))],
        out_specs=qspec,
        out_shape=jax.ShapeDtypeStruct((batch * seq, A_Q_HEADS * HEAD_DIM), BF16),
        compiler_params=_cparams(("arbitrary", "arbitrary", "arbitrary")), name="a_attn_prompt",
    )(q, k, v, bias)


def _a_score_sample_kernel(pt_ref, iq_ref, iw_ref, ikc_ref, ikn_ref, o_ref, *, n_pages):
    del pt_ref
    p = pl.program_id(1)
    ik = jnp.where(p == n_pages, ikn_ref[...], ikc_ref[...]).astype(BF16)
    iq = jnp.concatenate(
        [iq_ref[:, h * HEAD_DIM:(h + 1) * HEAD_DIM] for h in range(IDX_HEADS)], axis=0)
    rel = lax.dot_general(ik, iq.astype(BF16), (((1,), (1,)), ((), ())),
                          preferred_element_type=F32)
    w = jnp.maximum(rel, 0.0) * iw_ref[...]
    half = (IDX_HEADS * SUBLANES) // 2
    w = w[:, :half] + w[:, half:]
    shift = LANES // 2
    while shift >= SUBLANES:
        w = w + pltpu.roll(w, shift, 1)
        shift //= 2
    o_ref[...] = (w * IDX_SCALE).T[:SUBLANES, :]


def _a_score_sample(pt, iq_s, iw_row, cache_idx, ik_new, layer):
    nb, n_pages = pt.shape
    qrows = iq_s.shape[1]
    return pl.pallas_call(
        functools.partial(_a_score_sample_kernel, n_pages=n_pages),
        grid_spec=pltpu.PrefetchScalarGridSpec(
            num_scalar_prefetch=1, grid=(nb, n_pages + 1),
            in_specs=[pl.BlockSpec((None, qrows, IDX_HEADS * HEAD_DIM), lambda b, p, pt_r: (b, 0, 0)),
                      pl.BlockSpec((None, 1, IDX_HEADS * qrows), lambda b, p, pt_r: (b, 0, 0)),
                      pl.BlockSpec((None, None, PAGE_SIZE, HEAD_DIM),
                                   lambda b, p, pt_r: (layer, pt_r[b, jnp.minimum(p, n_pages - 1)], 0, 0)),
                      pl.BlockSpec((None, PAGE_SIZE, HEAD_DIM), lambda b, p, pt_r: (b, 0, 0))],
            out_specs=pl.BlockSpec((None, qrows, PAGE_SIZE), lambda b, p, pt_r: (b, 0, p))),
        out_shape=jax.ShapeDtypeStruct((nb, qrows, (n_pages + 1) * PAGE_SIZE), F32),
        compiler_params=_cparams(("arbitrary", "arbitrary")), name="a_score_sample",
    )(pt, iq_s, iw_row, cache_idx, ik_new)


def _a_select_sample_kernel(s_ref, o_ref, *, n_sel, pos0):
    q_pos = pos0 + lax.broadcasted_iota(I32, (s_ref.shape[0], 1), 0)
    o_ref[...] = _select_bias(s_ref[...], q_pos, n_sel)


def _a_select_sample(scores, *, n_sel, pos0):
    nb, qrows, nk = scores.shape
    spec = pl.BlockSpec((None, qrows, nk), lambda b: (b, 0, 0))
    return pl.pallas_call(
        functools.partial(_a_select_sample_kernel, n_sel=n_sel, pos0=pos0),
        grid=(nb,), in_specs=[spec], out_specs=spec,
        out_shape=jax.ShapeDtypeStruct(scores.shape, F32),
        compiler_params=_cparams(("arbitrary",)), name="a_select_sample",
    )(scores)


def _a_attn_sample_kernel(pt_ref, q_ref, bias_ref, kc_ref, vc_ref, kn_ref, vn_ref, o_ref,
                          m_ref, l_ref, acc_ref, *, n_pages, rep):
    del pt_ref
    p = pl.program_id(1)
    qrows = q_ref.shape[0]
    grows = rep * qrows

    @pl.when(p == 0)
    def _():
        m_ref[...] = jnp.full(m_ref.shape, -jnp.inf, F32)
        l_ref[...] = jnp.zeros(l_ref.shape, F32)
        acc_ref[...] = jnp.zeros(acc_ref.shape, F32)

    is_new = p == n_pages
    k = jnp.where(is_new, kn_ref[...], kc_ref[...]).astype(BF16)
    v = jnp.where(is_new, vn_ref[...], vc_ref[...]).astype(BF16)
    s_parts = []
    for g in range(A_KV_HEADS):
        qg = jnp.concatenate(
            [q_ref[:, (g * rep + r) * HEAD_DIM:(g * rep + r + 1) * HEAD_DIM] for r in range(rep)],
            axis=0).astype(BF16)
        s_parts.append(lax.dot_general(qg, k[:, g * HEAD_DIM:(g + 1) * HEAD_DIM],
                                       (((1,), (1,)), ((), ())), preferred_element_type=F32))
    bias = jnp.concatenate([bias_ref[...]] * (rep * A_KV_HEADS), axis=0)
    s = jnp.concatenate(s_parts, axis=0) * ATTN_SCALE + bias
    m_old = m_ref[...]
    m_new = jnp.maximum(m_old, jnp.max(s, axis=-1, keepdims=True))
    alpha = jnp.exp(m_old - m_new)
    pexp = jnp.exp(s - m_new)
    l_ref[...] = alpha * l_ref[...] + jnp.sum(pexp, axis=-1, keepdims=True)
    m_ref[...] = m_new
    pb = pexp.astype(BF16)
    pv = jnp.concatenate(
        [jnp.dot(pb[g * grows:(g + 1) * grows], v[:, g * HEAD_DIM:(g + 1) * HEAD_DIM],
                 preferred_element_type=F32) for g in range(A_KV_HEADS)], axis=0)
    acc_ref[...] = alpha * acc_ref[...] + pv

    @pl.when(is_new)
    def _():
        o = acc_ref[...] * (1.0 / l_ref[...])
        for hd in range(A_KV_HEADS * rep):
            o_ref[:, hd * HEAD_DIM:(hd + 1) * HEAD_DIM] = o[hd * qrows:(hd + 1) * qrows]


def _a_attn_sample(pt, q_s, bias, cache_k, cache_v, k_new, v_new, layer):
    nb, n_pages = pt.shape
    qrows = q_s.shape[1]
    rep = A_Q_HEADS // A_KV_HEADS
    kvw = A_KV_HEADS * HEAD_DIM
    page_spec = pl.BlockSpec((None, None, PAGE_SIZE, kvw),
                             lambda b, p, pt_r: (layer, pt_r[b, jnp.minimum(p, n_pages - 1)], 0, 0))
    new_spec = pl.BlockSpec((None, PAGE_SIZE, kvw), lambda b, p, pt_r: (b, 0, 0))
    qspec = pl.BlockSpec((None, qrows, A_Q_HEADS * HEAD_DIM), lambda b, p, pt_r: (b, 0, 0))
    stat = pltpu.VMEM((A_Q_HEADS * qrows, HEAD_DIM), F32)
    return pl.pallas_call(
        functools.partial(_a_attn_sample_kernel, n_pages=n_pages, rep=rep),
        grid_spec=pltpu.PrefetchScalarGridSpec(
            num_scalar_prefetch=1, grid=(nb, n_pages + 1),
            in_specs=[qspec,
                      pl.BlockSpec((None, qrows, PAGE_SIZE), lambda b, p, pt_r: (b, 0, p)),
                      page_spec, page_spec, new_spec, new_spec],
            out_specs=qspec,
            scratch_shapes=[stat, stat, stat]),
        out_shape=jax.ShapeDtypeStruct(q_s.shape, F32),
        compiler_params=_cparams(("arbitrary", "arbitrary")), name="a_attn_sample",
    )(pt, q_s, bias, cache_k, cache_v, k_new, v_new)


def _dilated_bias(dist, window, dilation):
    z = dist if dilation == 1 else dist | lax.shift_left(dist & (dilation - 1), 16)
    return jnp.where(z >= 0, jnp.where(z <= window, 0.0, MASK_BIAS), MASK_BIAS)


def _dilated_merge(s_list, v_list, bias_list, rows, rep):
    m = None
    s_b = []
    for s, bias in zip(s_list, bias_list):
        sb = s * ATTN_SCALE + jnp.concatenate([bias] * rep, axis=0)
        s_b.append(sb)
        mg = jnp.max(sb, axis=-1, keepdims=True)
        m = mg if m is None else jnp.maximum(m, mg)
    l = jnp.zeros_like(m)
    o = None
    for sb, v in zip(s_b, v_list):
        p = jnp.exp(sb - m)
        l = l + jnp.sum(p, axis=-1, keepdims=True)
        pv = jnp.dot(p.astype(BF16), v, preferred_element_type=F32)
        o = pv if o is None else o + pv
    return o * (1.0 / l)


def _b_attn_prompt_kernel(q1_ref, q2_ref, q3_ref, k_ref, v_ref, o_ref, *, rep, groups):
    seq = k_ref.shape[0]
    rows = q1_ref.shape[0]
    t0 = pl.program_id(2) * rows
    s_list, v_list, bias_list = [], [], []
    for q_ref, (window, dilation) in zip((q1_ref, q2_ref, q3_ref), groups):
        size = min(window + rows, seq)
        if size == seq:
            start = 0
            kw, vw = k_ref[...], v_ref[...]
        else:
            start = pl.multiple_of(jnp.maximum(t0 - window, 0), rows)
            kw, vw = k_ref[pl.ds(start, size), :], v_ref[pl.ds(start, size), :]
        s_list.append(lax.dot_general(_stack_heads(q_ref, rep), kw.astype(BF16),
                                      (((1,), (1,)), ((), ())), preferred_element_type=F32))
        v_list.append(vw.astype(BF16))
        dist = (t0 + lax.broadcasted_iota(I32, (rows, size), 0)) - \
               (start + lax.broadcasted_iota(I32, (rows, size), 1))
        bias_list.append(_dilated_bias(dist, window, dilation))
    o = _dilated_merge(s_list, v_list, bias_list, rows, rep)
    for r in range(rep):
        o_ref[:, r * HEAD_DIM:(r + 1) * HEAD_DIM] = o[r * rows:(r + 1) * rows].astype(o_ref.dtype)


def _b_attn_prompt(q3, k, v, *, batch, seq):
    nqb = seq // Q_BLOCK
    rep = B_Q_HEADS // B_KV_HEADS
    w = rep * HEAD_DIM

    def qspec(g):
        return pl.BlockSpec((Q_BLOCK, w), lambda b, h, q: (b * nqb + q, g * B_KV_HEADS + h))

    kvspec = pl.BlockSpec((seq, HEAD_DIM), lambda b, h, q: (b, h))
    return pl.pallas_call(
        functools.partial(_b_attn_prompt_kernel, rep=rep, groups=DILATED_GROUPS),
        grid=(batch, B_KV_HEADS, nqb),
        in_specs=[qspec(0), qspec(1), qspec(2), kvspec, kvspec],
        out_specs=pl.BlockSpec((Q_BLOCK, w), lambda b, h, q: (b * nqb + q, h)),
        out_shape=jax.ShapeDtypeStruct((batch * seq, B_Q_HEADS * HEAD_DIM), BF16),
        compiler_params=_cparams(("arbitrary", "arbitrary", "arbitrary")), name="b_attn_prompt",
    )(q3, q3, q3, k, v)


def _b_attn_sample_kernel(q1_ref, q2_ref, q3_ref, k_ref, v_ref, o_ref, *, rep, groups, q_idx0):
    rows = q1_ref.shape[0]
    n_ctx = k_ref.shape[0]
    k = k_ref[...].astype(BF16)
    v = v_ref[...].astype(BF16)
    dist = (q_idx0 + lax.broadcasted_iota(I32, (rows, n_ctx), 0)) - \
        lax.broadcasted_iota(I32, (rows, n_ctx), 1)
    s_list, bias_list = [], []
    for q_ref, (window, dilation) in zip((q1_ref, q2_ref, q3_ref), groups):
        s_list.append(lax.dot_general(_stack_heads(q_ref, rep), k, (((1,), (1,)), ((), ())),
                                      preferred_element_type=F32))
        bias_list.append(_dilated_bias(dist, window, dilation))
    o = _dilated_merge(s_list, [v] * len(groups), bias_list, rows, rep)
    for r in range(rep):
        o_ref[:, r * HEAD_DIM:(r + 1) * HEAD_DIM] = o[r * rows:(r + 1) * rows]


def _b_attn_sample(q3_s, ctx_k, ctx_v, *, q_idx0):
    nb, qrows, _ = q3_s.shape
    n_ctx = ctx_k.shape[1]
    rep = B_Q_HEADS // B_KV_HEADS
    w = rep * HEAD_DIM

    def qspec(g):
        return pl.BlockSpec((None, qrows, w), lambda b, h: (b, 0, g * B_KV_HEADS + h))

    kvspec = pl.BlockSpec((None, n_ctx, HEAD_DIM), lambda b, h: (b, 0, h))
    return pl.pallas_call(
        functools.partial(_b_attn_sample_kernel, rep=rep, groups=DILATED_GROUPS, q_idx0=q_idx0),
        grid=(nb, B_KV_HEADS),
        in_specs=[qspec(0), qspec(1), qspec(2), kvspec, kvspec],
        out_specs=pl.BlockSpec((None, qrows, w), lambda b, h: (b, 0, h)),
        out_shape=jax.ShapeDtypeStruct((nb, qrows, B_Q_HEADS * HEAD_DIM), F32),
        compiler_params=_cparams(("arbitrary", "arbitrary")), name="b_attn_sample",
    )(q3_s, q3_s, q3_s, ctx_k, ctx_v)


def _router_kernel(x_ref, w_ref, o_ref):
    logits = jnp.dot(x_ref[...], w_ref[...], preferred_element_type=F32,
                     precision=lax.Precision.HIGHEST)
    lane = lax.broadcasted_iota(I32, logits.shape, 1).astype(F32)
    lg = jnp.where(lane < N_EXPERTS, logits, -jnp.inf)
    m1 = jnp.max(lg, axis=-1, keepdims=True)
    i1 = jnp.min(jnp.where(lg == m1, lane, float(LANES)), axis=-1, keepdims=True)
    lg2 = jnp.where(lane == i1, -jnp.inf, lg)
    m2 = jnp.max(lg2, axis=-1, keepdims=True)
    i2 = jnp.min(jnp.where(lg2 == m2, lane, float(LANES)), axis=-1, keepdims=True)
    e = jnp.exp(m2 - m1)
    g1 = 1.0 / (1.0 + e)
    g2 = e * g1
    o_ref[...] = jnp.where(lane == 0, g1, jnp.where(lane == 1, g2, jnp.where(
        lane == 2, i1, jnp.where(lane == 3, i2, 0.0))))


def _router(h, w_router_pad, layer):
    m, d = h.shape
    return pl.pallas_call(
        _router_kernel, grid=(m // TM_ROW,),
        in_specs=[pl.BlockSpec((TM_ROW, d), lambda i: (i, 0)),
                  pl.BlockSpec((None, d, LANES), lambda i: (layer, 0, 0))],
        out_specs=pl.BlockSpec((TM_ROW, LANES), lambda i: (i, 0)),
        out_shape=jax.ShapeDtypeStruct((m, LANES), F32),
        compiler_params=_cparams(("arbitrary",)), name="router",
    )(h, w_router_pad)


def _row_copy(src_hbm, row, dst, dst_row, sem):
    return pltpu.make_async_copy(src_hbm.at[pl.ds(row, 1)], dst.at[pl.ds(dst_row, 1)], sem)


def _dispatch_kernel(live_ref, tok_ref, h_hbm, o_ref, buf, sem):
    i = pl.program_id(0)
    rows = buf.shape[0]

    @pl.when(live_ref[i] > 0)
    def _():
        def start(r, c):
            _row_copy(h_hbm, tok_ref[0, r], buf, r, sem).start()
            return c

        def wait(r, c):
            _row_copy(h_hbm, 0, buf, r, sem).wait()
            return c

        lax.fori_loop(0, rows, start, 0)
        lax.fori_loop(0, rows, wait, 0)
        o_ref[...] = buf[...].astype(o_ref.dtype)

    @pl.when(live_ref[i] == 0)
    def _():
        o_ref[...] = jnp.zeros(o_ref.shape, o_ref.dtype)


def _dispatch(h, tok_of, tile_live):
    d = h.shape[1]
    n_rows = tok_of.shape[0]
    return pl.pallas_call(
        _dispatch_kernel,
        grid_spec=pltpu.PrefetchScalarGridSpec(
            num_scalar_prefetch=1, grid=(n_rows // TM_ROW,),
            in_specs=[pl.BlockSpec((None, 1, TM_ROW), lambda i, live: (i, 0, 0),
                                   memory_space=pltpu.SMEM),
                      pl.BlockSpec(memory_space=pl.ANY)],
            out_specs=pl.BlockSpec((TM_ROW, d), lambda i, live: (i, 0)),
            scratch_shapes=[pltpu.VMEM((TM_ROW, d), F32), pltpu.SemaphoreType.DMA(())]),
        out_shape=jax.ShapeDtypeStruct((n_rows, d), BF16),
        compiler_params=_cparams(("arbitrary",)), name="moe_dispatch",
    )(tile_live, tok_of.reshape(n_rows // TM_ROW, 1, TM_ROW), h)


def _combine_kernel(pos_ref, y_hbm, h_ref, gate_ref, g_ref, b_ref, o_ref, ob_ref, buf, sem):
    rows = h_ref.shape[0]

    def start(r, c):
        for kk in range(2):
            _row_copy(y_hbm, pos_ref[0, 2 * r + kk], buf.at[kk], r, sem).start()
        return c

    def wait(r, c):
        for kk in range(2):
            _row_copy(y_hbm, 0, buf.at[kk], r, sem).wait()
        return c

    lax.fori_loop(0, rows, start, 0)
    lax.fori_loop(0, rows, wait, 0)
    gates = gate_ref[...]
    ffn = gates[:, 0:1] * buf[0] + gates[:, 1:2] * buf[1]
    o = _layer_norm_rows(DEEPNORM_ALPHA * h_ref[...] + ffn, g_ref[...], b_ref[...])
    o_ref[...] = o
    ob_ref[...] = o.astype(BF16)


def _combine_deepnorm(h, y_sorted, pos, gates, g, b):
    m, d = h.shape
    row = pl.BlockSpec((TM_ROW, d), lambda i: (i, 0))
    vec = pl.BlockSpec((1, d), lambda i: (0, 0))
    return pl.pallas_call(
        _combine_kernel,
        grid=(m // TM_ROW,),
        in_specs=[pl.BlockSpec((None, 1, 2 * TM_ROW), lambda i: (i, 0, 0), memory_space=pltpu.SMEM),
                  pl.BlockSpec(memory_space=pl.ANY), row,
                  pl.BlockSpec((TM_ROW, LANES), lambda i: (i, 0)), vec, vec],
        out_specs=[row, row],
        scratch_shapes=[pltpu.VMEM((2, TM_ROW, d), F32), pltpu.SemaphoreType.DMA(())],
        out_shape=[jax.ShapeDtypeStruct((m, d), F32), jax.ShapeDtypeStruct((m, d), BF16)],
        compiler_params=_cparams(("arbitrary",)), name="moe_combine",
    )(pos.reshape(m // TM_ROW, 1, 2 * TM_ROW), y_sorted, h, gates, g.reshape(1, d), b.reshape(1, d))


def _moe_plan(top_idx, n_tiles_max):
    n_assign = top_idx.size
    e_flat = top_idx.reshape(-1)
    counts = jnp.zeros((N_EXPERTS,), I32).at[e_flat].add(1)
    tiles_e = (counts + TM_MOE - 1) // TM_MOE
    tile_end = jnp.cumsum(tiles_e)
    tile_start = tile_end - tiles_e
    order = jnp.argsort(e_flat, stable=True).astype(I32)
    e_sorted = e_flat[order]
    group_start = jnp.cumsum(counts) - counts
    pos_sorted = tile_start[e_sorted] * TM_MOE + (jnp.arange(n_assign, dtype=I32) - group_start[e_sorted])
    pos = jnp.zeros((n_assign,), I32).at[order].set(pos_sorted)
    n_rows = n_tiles_max * TM_MOE
    tok_of = jnp.zeros((n_rows,), I32).at[pos_sorted].set(order // 2)
    n_used = tile_end[-1]
    t = jnp.arange(n_tiles_max, dtype=I32)
    gid = jnp.minimum(jnp.searchsorted(tile_end, t, side="right").astype(I32), N_EXPERTS - 1)
    nv = jnp.clip(counts[gid] - (t - tile_start[gid]) * TM_MOE, 0, TM_MOE)
    nv = jnp.where(t < n_used, nv, 0).astype(I32)
    gid = jnp.where(t < n_used, gid, gid[jnp.maximum(n_used - 1, 0)])
    sub = jnp.arange(n_rows // TM_ROW, dtype=I32)
    live = (nv[sub * TM_ROW // TM_MOE] > (sub * TM_ROW) % TM_MOE).astype(I32)
    return pos, tok_of, gid, nv, n_used.astype(I32), live


def _rope_tables(pos):
    half = HEAD_DIM // 2
    inv_freq = jnp.power(ROPE_THETA, -jnp.arange(half, dtype=F32) / half)
    ang = pos.astype(F32)[:, None] * inv_freq[None, :]
    cos, sin = jnp.cos(ang), jnp.sin(ang)
    return jnp.concatenate([cos, cos], axis=1), jnp.concatenate([-sin, sin], axis=1)


def _sample_rows(a, n_prompt, nb, nq, qrows):
    x = a[n_prompt:n_prompt + nb * nq].reshape(nb, nq, a.shape[1]).astype(F32)
    return jnp.pad(x, ((0, 0), (0, qrows - nq), (0, 0)))


def _as_page(rows_s, nb, nq):
    x = rows_s.reshape(nb, nq, rows_s.shape[1])
    return jnp.pad(x, ((0, 0), (0, PAGE_SIZE - nq), (0, 0)))


def kernel(x_prompt, x_sample, cache_a_k, cache_a_v, cache_a_idx, state_b_k, state_b_v, page_table,
           w_a_in, w_a_out, w_kv_shared, w_b_q, w_b_out, ln_mix_g, ln_mix_b, ln_ffn_g, ln_ffn_b,
           w_ffn_in, w_ffn_down, w_router, w_exp_in, w_exp_down):
    batch, seq, d = x_prompt.shape
    nb, nq, _ = x_sample.shape
    n_prompt, n_samp = batch * seq, nb * nq
    past_len = page_table.shape[1] * PAGE_SIZE
    n_a_layers = w_a_in.shape[0]
    m_pad = pl.cdiv(n_prompt + n_samp, TM_DENSE) * TM_DENSE
    assert m_pad % TM_ROW == 0 and n_prompt % TM_ROW == 0
    qrows = SUBLANES
    n_sel_p = min(IDX_TOPK_MAX, seq // 4)
    n_sel_s = min(IDX_TOPK_MAX, (past_len + nq) // 4)
    w_buf = state_b_k.shape[1]

    pos_all = jnp.concatenate([jnp.tile(jnp.arange(seq, dtype=I32), batch),
                               jnp.tile(past_len + jnp.arange(nq, dtype=I32), nb),
                               jnp.zeros((m_pad - n_prompt - n_samp,), I32)])
    tabs = _rope_tables(pos_all)

    h = jnp.concatenate([x_prompt.reshape(n_prompt, d), x_sample.reshape(n_samp, d),
                         jnp.zeros((m_pad - n_prompt - n_samp, d), F32)], axis=0)
    hb = h.astype(BF16)

    qw = A_Q_HEADS * HEAD_DIM
    kvw = A_KV_HEADS * HEAD_DIM
    iqw = IDX_HEADS * HEAD_DIM
    cache_k2 = cache_a_k.reshape(cache_a_k.shape[:3] + (kvw,))
    cache_v2 = cache_a_v.reshape(cache_a_v.shape[:3] + (kvw,))
    w_router_pad = jnp.pad(w_router, ((0, 0), (0, 0), (0, LANES - N_EXPERTS)))
    w_exp_in2 = w_exp_in.reshape((-1,) + w_exp_in.shape[2:])
    w_exp_down2 = w_exp_down.reshape((-1,) + w_exp_down.shape[2:])
    w_kv3 = w_kv_shared[None]
    w_iw = w_a_in[:, :, qw + 2 * kvw + iqw + HEAD_DIM:]
    n_tiles_moe = (2 * m_pad) // TM_MOE + N_EXPERTS

    def pad_rows(mix_p, mix_s):
        return jnp.concatenate([mix_p, mix_s[:, :nq].reshape(n_samp, -1).astype(BF16),
                                jnp.zeros((m_pad - n_prompt - n_samp, mix_p.shape[1]), BF16)], axis=0)

    ak, av, ai = [], [], []
    kb = vb = ctx_k = ctx_v = None
    for l in range(DEPTH):
        if l < n_a_layers:
            proj = functools.partial(_dense, hb, w_a_in, l, tm=TM_DENSE)
            q = proj(col0=0, n_out=qw, tn=TN_PROJ, mode="rope", tabs=tabs, name="a_q")
            k = proj(col0=qw, n_out=kvw, tn=TN_PROJ, mode="rope", tabs=tabs, name="a_k")
            v = proj(col0=qw + kvw, n_out=kvw, tn=TN_PROJ, name="a_v")
            iq = proj(col0=qw + 2 * kvw, n_out=iqw, tn=TN_PROJ, mode="rope", tabs=tabs, name="a_iq")
            ik = proj(col0=qw + 2 * kvw + iqw, n_out=HEAD_DIM, tn=LANES, mode="rope", tabs=tabs,
                      name="a_ik")
            iw = _dense(hb, w_iw, l, col0=0, n_out=IDX_HEADS, tm=TM_DENSE, tn=IDX_HEADS, name="a_iw")
            ak.append(k)
            av.append(v)
            ai.append(ik)
            bias_p = _a_select_prompt(iq, ik, iw.T, batch=batch, seq=seq, n_sel=n_sel_p)
            mix_p = _a_attn_prompt(q, k, v, bias_p, batch=batch, seq=seq)
            iq_s = _sample_rows(iq, n_prompt, nb, nq, qrows)
            iw_s = _sample_rows(iw, n_prompt, nb, nq, qrows)
            iw_row = jnp.transpose(iw_s, (0, 2, 1)).reshape(nb, 1, IDX_HEADS * qrows)
            scores = _a_score_sample(page_table, iq_s, iw_row, cache_a_idx,
                                     _as_page(ik[n_prompt:n_prompt + n_samp], nb, nq), l)
            bias_s = _a_select_sample(scores, n_sel=n_sel_s, pos0=past_len)
            mix_s = _a_attn_sample(page_table, _sample_rows(q, n_prompt, nb, nq, qrows), bias_s,
                                   cache_k2, cache_v2,
                                   _as_page(k[n_prompt:n_prompt + n_samp], nb, nq),
                                   _as_page(v[n_prompt:n_prompt + n_samp], nb, nq), l)
            w_out, lo = w_a_out, l
        else:
            j = l - n_a_layers
            q3 = _dense(hb, w_b_q, j, col0=0, n_out=w_b_q.shape[2], tm=TM_DENSE, tn=TN_PROJ,
                        mode="rope", tabs=tabs, name="b_q")
            mix_p = _b_attn_prompt(q3, kb, vb, batch=batch, seq=seq)
            mix_s = _b_attn_sample(_sample_rows(q3, n_prompt, nb, nq, qrows), ctx_k, ctx_v,
                                   q_idx0=w_buf)
            w_out, lo = w_b_out, j
        sub = _dense(pad_rows(mix_p, mix_s), w_out, lo, col0=0, n_out=d, tm=TM_DENSE, tn=TN_PROJ,
                     name="mix_out")
        h, hb = _deepnorm(h, sub, ln_mix_g[l], ln_mix_b[l])

        f = l // 2
        if l % 2 == 0:
            act = _dense(hb, w_ffn_in, f, col0=0, n_out=D_FF, up_col0=D_FF, tm=TM_DENSE, tn=TN_FFN,
                         mode="swiglu", out_dtype=BF16, name="ffn_in")
            sub = _dense(act, w_ffn_down, f, col0=0, n_out=d, tm=TM_DENSE, tn=1024, tk=1024,
                         name="ffn_down")
            h, hb = _deepnorm(h, sub, ln_ffn_g[l], ln_ffn_b[l])
        else:
            route = _router(h, w_router_pad, f)
            gates = route
            top_idx = route[:, 2:4].astype(I32)
            pos, tok_of, gid, nv, n_used, live = _moe_plan(top_idx, n_tiles_moe)
            gid = gid + f * N_EXPERTS
            xs = _dispatch(h, tok_of, live)
            act = _matmul(xs, w_exp_in2, gid, nv, n_used, col0=0, n_out=D_FF, up_col0=D_FF,
                          tm=TM_MOE, tn=TN_FFN, mode="swiglu", out_dtype=BF16, name="moe_in")
            ys = _matmul(act, w_exp_down2, gid, nv, n_used, col0=0, n_out=d, tm=TM_MOE, tn=1024,
                         tk=1024, name="moe_down")
            h, hb = _combine_deepnorm(h, ys, pos, gates, ln_ffn_g[l], ln_ffn_b[l])

        if l == n_a_layers - 1:
            kb = _dense(hb, w_kv3, 0, col0=0, n_out=kvw, tm=TM_DENSE, tn=TN_PROJ, mode="rope",
                        tabs=tabs, name="kv_k")
            vb = _dense(hb, w_kv3, 0, col0=kvw, n_out=kvw, tm=TM_DENSE, tn=TN_PROJ, name="kv_v")
            kb_new = kb[n_prompt:n_prompt + n_samp].reshape(nb, nq, kvw)
            vb_new = vb[n_prompt:n_prompt + n_samp].reshape(nb, nq, kvw)
            ctx_pad = (-(w_buf + nq)) % LANES
            ctx_k = jnp.concatenate([state_b_k.reshape(nb, w_buf, kvw), kb_new,
                                     jnp.zeros((nb, ctx_pad, kvw), F32)], axis=1)
            ctx_v = jnp.concatenate([state_b_v.reshape(nb, w_buf, kvw), vb_new,
                                     jnp.zeros((nb, ctx_pad, kvw), F32)], axis=1)

    def heads(a, lead):
        return a.reshape(lead + (A_KV_HEADS, HEAD_DIM))

    y_prompt = h[:n_prompt].reshape(batch, seq, d)
    y_sample = h[n_prompt:n_prompt + n_samp].reshape(nb, nq, d)
    new_a_k_prompt = jnp.stack([heads(a[:n_prompt], (batch, seq)) for a in ak])
    new_a_v_prompt = jnp.stack([heads(a[:n_prompt], (batch, seq)) for a in av])
    new_a_idx_prompt = jnp.stack([a[:n_prompt].reshape(batch, seq, HEAD_DIM) for a in ai])
    new_a_k_sample = jnp.stack([heads(a[n_prompt:n_prompt + n_samp], (nb, nq)) for a in ak])
    new_a_v_sample = jnp.stack([heads(a[n_prompt:n_prompt + n_samp], (nb, nq)) for a in av])
    new_a_idx_sample = jnp.stack([a[n_prompt:n_prompt + n_samp].reshape(nb, nq, HEAD_DIM) for a in ai])
    w_keep = min(2048, seq)
    new_b_k_prompt = heads(kb[:n_prompt], (batch, seq))[:, seq - w_keep:]
    new_b_v_prompt = heads(vb[:n_prompt], (batch, seq))[:, seq - w_keep:]
    new_b_k_sample = heads(ctx_k[:, nq:nq + w_buf], (nb, w_buf))
    new_b_v_sample = heads(ctx_v[:, nq:nq + w_buf], (nb, w_buf))
    return (y_prompt, y_sample, new_a_k_prompt, new_a_v_prompt, new_a_idx_prompt, new_a_k_sample,
            new_a_v_sample, new_a_idx_sample, new_b_k_prompt, new_b_v_prompt, new_b_k_sample,
            new_b_v_sample)
```

```python
import functools
import math

import jax
import jax.numpy as jnp
from jax import lax
from jax.experimental import pallas as pl
from jax.experimental.pallas import tpu as pltpu

F32 = jnp.float32
BF16 = jnp.bfloat16
I32 = jnp.int32

LANES = 128
SUBLANES = 8
VMEM_LIMIT_BYTES = 56 * 1024 * 1024

D_MODEL = 4096
HEAD_DIM = 128
PAGE_SIZE = 128
A_Q_HEADS = 32
A_KV_HEADS = 8
IDX_HEADS = 32
IDX_TOPK_MAX = 256
B_Q_HEADS = 32
B_KV_HEADS = 8
DILATED_GROUPS = ((128, 1), (512, 4), (2048, 16))
N_EXPERTS = 8
D_FF = 14336
ROPE_THETA = 10000.0
Q_BLOCK = 128
LN_EPS = 1e-5
DEPTH = 4
DEEPNORM_ALPHA = (2.0 * DEPTH) ** 0.25
ATTN_SCALE = HEAD_DIM ** -0.5
IDX_SCALE = (IDX_HEADS ** -0.5) * (HEAD_DIM ** -0.5)
MASK_BIAS = -1e30
INT_MIN = -(2 ** 31)

TM_DENSE = 832
TM_MOE = 1024
TM_ROW = 128
TN_PROJ = 512
TN_FFN = 256
TN_DOWN = 2048
TK_DOWN = 512
MOE_PATHS = 4
KEY_CHUNK = 512
B_TAPS = 128
B_QUERY_BLOCK = 512


def _cparams(sem):
    return pltpu.CompilerParams(dimension_semantics=sem, vmem_limit_bytes=VMEM_LIMIT_BYTES)


def _rope_tile(y, cos, sin_signed):
    outs = []
    for g in range(y.shape[1] // HEAD_DIM):
        yh = y[:, g * HEAD_DIM:(g + 1) * HEAD_DIM]
        outs.append(yh * cos + pltpu.roll(yh, HEAD_DIM // 2, 1) * sin_signed)
    return outs[0] if len(outs) == 1 else jnp.concatenate(outs, axis=1)


def _mm_rows(refs, rows, mode, nk, k):
    o_ref = refs[-1]
    tm = o_ref.shape[0]
    if mode == "swiglu":
        x_ref, wg_ref, wu_ref, _ = refs
        x = x_ref[0:rows, :]
        g = jnp.dot(x, wg_ref[...].astype(BF16), preferred_element_type=F32)
        u = jnp.dot(x, wu_ref[...].astype(BF16), preferred_element_type=F32)
        o_ref[0:rows, :] = (g * (1.0 / (1.0 + jnp.exp(-g))) * u).astype(o_ref.dtype)
    elif mode == "rope":
        x_ref, w_ref, cos_ref, sin_ref, _ = refs
        y = jnp.dot(x_ref[0:rows, :], w_ref[...].astype(BF16), preferred_element_type=F32)
        o_ref[0:rows, :] = _rope_tile(y, cos_ref[0:rows, :], sin_ref[0:rows, :]).astype(o_ref.dtype)
    else:
        x_ref, w_ref, _ = refs

        def dot():
            return jnp.dot(x_ref[0:rows, :], w_ref[...].astype(BF16), preferred_element_type=F32)

        if nk == 1:
            o_ref[0:rows, :] = dot().astype(o_ref.dtype)
        else:
            @pl.when(k == 0)
            def _():
                o_ref[0:rows, :] = dot()

            @pl.when(k > 0)
            def _():
                o_ref[0:rows, :] += dot()
    if rows < tm:
        o_ref[rows:tm, :] = jnp.zeros((tm - rows, o_ref.shape[1]), o_ref.dtype)


def _mm_kernel(meta_ref, gid_ref, nv_ref, *refs, mode, nk, n_paths):
    del meta_ref, gid_ref
    o_ref = refs[-1]
    k = pl.program_id(2)
    step = o_ref.shape[0] // n_paths
    n_sub = (nv_ref[pl.program_id(0)] + step - 1) // step
    for s in range(1, n_paths + 1):
        pl.when(n_sub == s)(functools.partial(_mm_rows, refs, s * step, mode, nk, k))

    @pl.when(n_sub == 0)
    def _():
        o_ref[...] = jnp.zeros(o_ref.shape, o_ref.dtype)


def _matmul(x, w, gid, nv, n_used, *, col0, n_out, tm, tn, tk=None, mode="plain",
            tabs=None, out_dtype=F32, up_col0=None, n_paths=1, name="mm"):
    m, kdim = x.shape
    tk = kdim if tk is None else tk
    nk = kdim // tk
    assert m % tm == 0 and kdim % tk == 0 and col0 % tn == 0
    assert tm % (n_paths * 2 * SUBLANES) == 0
    assert mode == "plain" or nk == 1
    assert out_dtype == F32 or nk == 1
    cb0 = col0 // tn
    grid = (m // tm, pl.cdiv(n_out, tn), nk)

    def x_map(i, j, k, meta, gid_r, nv_r):
        return (jnp.minimum(i, meta[0] - 1), k)

    def w_map_at(cb):
        def w_map(i, j, k, meta, gid_r, nv_r):
            live = nv_r[i] > 0
            return (gid_r[i], jnp.where(live, k, 0), cb + jnp.where(live, j, 0))
        return w_map

    def row_map(i, j, k, meta, gid_r, nv_r):
        return (i, 0)

    def o_map(i, j, k, meta, gid_r, nv_r):
        return (i, j)

    in_specs = [pl.BlockSpec((tm, tk), x_map), pl.BlockSpec((None, tk, tn), w_map_at(cb0))]
    args = [x, w]
    if mode == "swiglu":
        assert up_col0 % tn == 0
        in_specs.append(pl.BlockSpec((None, tk, tn), w_map_at(up_col0 // tn)))
        args.append(w)
    if mode == "rope":
        in_specs += [pl.BlockSpec((tm, HEAD_DIM), row_map)] * 2
        args += list(tabs)
    meta = jnp.reshape(n_used, (1,)).astype(I32)
    return pl.pallas_call(
        functools.partial(_mm_kernel, mode=mode, nk=nk, n_paths=n_paths),
        grid_spec=pltpu.PrefetchScalarGridSpec(
            num_scalar_prefetch=3, grid=grid, in_specs=in_specs,
            out_specs=pl.BlockSpec((tm, tn), o_map)),
        out_shape=jax.ShapeDtypeStruct((m, n_out), out_dtype),
        compiler_params=_cparams(("arbitrary", "arbitrary", "arbitrary")),
        name=name,
    )(meta, gid, nv, *args)


def _dense(x, w, layer, **kw):
    nt = x.shape[0] // kw["tm"]
    gid = jnp.full((nt,), layer, I32)
    nv = jnp.full((nt,), kw["tm"], I32)
    return _matmul(x, w, gid, nv, jnp.asarray(nt, I32), **kw)


def _layer_norm_rows(y, g, b):
    mu = jnp.mean(y, axis=-1, keepdims=True)
    d = y - mu
    var = jnp.mean(d * d, axis=-1, keepdims=True)
    return d * lax.rsqrt(var + LN_EPS) * g + b


def _deepnorm_kernel(h_ref, s_ref, g_ref, b_ref, o_ref, ob_ref):
    y = DEEPNORM_ALPHA * h_ref[...] + s_ref[...]
    o = _layer_norm_rows(y, g_ref[...], b_ref[...])
    o_ref[...] = o
    ob_ref[...] = o.astype(BF16)


def _deepnorm(h, sub, g, b):
    m, d = h.shape
    row = pl.BlockSpec((TM_ROW, d), lambda i: (i, 0))
    vec = pl.BlockSpec((1, d), lambda i: (0, 0))
    return pl.pallas_call(
        _deepnorm_kernel, grid=(m // TM_ROW,),
        in_specs=[row, row, vec, vec], out_specs=[row, row],
        out_shape=[jax.ShapeDtypeStruct((m, d), F32), jax.ShapeDtypeStruct((m, d), BF16)],
        compiler_params=_cparams(("arbitrary",)), name="deepnorm",
    )(h, sub, g.reshape(1, d), b.reshape(1, d))


def _row_count(mask):
    return jnp.sum(jnp.where(mask, 1.0, 0.0), axis=-1, keepdims=True)


def _select_bias(score, q_pos, n_sel):
    r_rows, n_keys = score.shape
    idx = lax.broadcasted_iota(I32, (r_rows, n_keys), 1)
    adm = idx <= q_pos
    sc = jnp.where(adm, score + 0.0, -jnp.inf)
    bits = pltpu.bitcast(sc, I32)
    key = bits ^ ((bits >> 31) & 0x7FFFFFFF)
    kf = float(n_sel)

    t0 = jnp.where(_row_count(key >= 0) >= kf, 0, INT_MIN).astype(I32)

    def value_step(i, t):
        cand = t | lax.shift_left(jnp.int32(1), jnp.int32(30) - i)
        return jnp.where(_row_count(key >= cand) >= kf, cand, t)

    thr = lax.fori_loop(0, 31, value_step, t0)
    gt = key > thr
    need = kf - _row_count(gt)
    not_tie = jnp.int32(2 ** 30)
    tie_idx = jnp.where(key == thr, idx, not_tie)
    n_bits = n_keys.bit_length()

    def index_step(i, j0):
        cand = j0 + lax.shift_left(jnp.int32(1), jnp.int32(n_bits - 1) - i)
        return jnp.where(_row_count(tie_idx < cand) < need, cand, j0)

    def lowest_ties():
        return lax.fori_loop(0, n_bits, index_step, jnp.zeros((r_rows, 1), I32))

    surplus = jnp.max(_row_count(key >= thr)) > kf
    j0 = lax.cond(surplus, lowest_ties, lambda: jnp.full((r_rows, 1), not_tie - 1, I32))
    picked = jnp.where(gt, 0.0, jnp.where(tie_idx <= j0, 0.0, MASK_BIAS))
    return jnp.where(adm, picked, MASK_BIAS)


def _a_select_prompt_kernel(iq_ref, ik_ref, iwt_ref, bias_ref, *, n_sel):
    qb = pl.program_id(1)
    ik = ik_ref[...].astype(BF16)
    acc = jnp.zeros((ik.shape[0], Q_BLOCK), F32)
    for h in range(IDX_HEADS):
        iqh = iq_ref[:, h * HEAD_DIM:(h + 1) * HEAD_DIM].astype(BF16)
        rel = lax.dot_general(ik, iqh, (((1,), (1,)), ((), ())), preferred_element_type=F32)
        acc = acc + jnp.maximum(rel, 0.0) * iwt_ref[h:h + 1, :]
    score = (acc * IDX_SCALE).T
    q_pos = qb * Q_BLOCK + lax.broadcasted_iota(I32, (Q_BLOCK, 1), 0)
    bias = _select_bias(score, q_pos, n_sel).astype(bias_ref.dtype)
    chunk = bias_ref.shape[2]
    for c in range(bias_ref.shape[0]):
        bias_ref[c] = bias[:, c * chunk:(c + 1) * chunk]


def _a_select_prompt(iq, ik, iwt, *, batch, seq, n_sel):
    nqb = seq // Q_BLOCK
    chunk = min(KEY_CHUNK, seq)
    return pl.pallas_call(
        functools.partial(_a_select_prompt_kernel, n_sel=n_sel),
        grid=(batch, nqb),
        in_specs=[pl.BlockSpec((Q_BLOCK, IDX_HEADS * HEAD_DIM), lambda b, q: (b * nqb + q, 0)),
                  pl.BlockSpec((seq, HEAD_DIM), lambda b, q: (b, 0)),
                  pl.BlockSpec((IDX_HEADS, Q_BLOCK), lambda b, q: (0, b * nqb + q))],
        out_specs=pl.BlockSpec((None, seq // chunk, Q_BLOCK, chunk), lambda b, q: (b * nqb + q, 0, 0, 0)),
        out_shape=jax.ShapeDtypeStruct((batch * nqb, seq // chunk, Q_BLOCK, chunk), BF16),
        compiler_params=_cparams(("arbitrary", "arbitrary")), name="a_select_prompt",
    )(iq, ik, iwt)


def _stack_heads(q_ref, rep):
    return jnp.concatenate(
        [q_ref[:, r * HEAD_DIM:(r + 1) * HEAD_DIM] for r in range(rep)], axis=0).astype(BF16)


def _a_attn_prompt_kernel(q_ref, k_ref, v_ref, bias_ref, o_ref, m_ref, l_ref, acc_ref, *, rep):
    rows = q_ref.shape[0]
    chunk = bias_ref.shape[2]
    n_chunks = (pl.program_id(2) * rows + rows + chunk - 1) // chunk
    qs = _stack_heads(q_ref, rep)
    m_ref[...] = jnp.full(m_ref.shape, -jnp.inf, F32)
    l_ref[...] = jnp.zeros(l_ref.shape, F32)
    acc_ref[...] = jnp.zeros(acc_ref.shape, F32)

    def step(c, carry):
        start = pl.multiple_of(c * chunk, chunk)
        k = k_ref[pl.ds(start, chunk), :].astype(BF16)
        v = v_ref[pl.ds(start, chunk), :].astype(BF16)
        bias = bias_ref[c].astype(F32)
        s = lax.dot_general(qs, k, (((1,), (1,)), ((), ())), preferred_element_type=F32)
        s = s * ATTN_SCALE + jnp.concatenate([bias] * rep, axis=0)
        m_old = m_ref[...]
        m_new = jnp.maximum(m_old, jnp.max(s, axis=-1, keepdims=True))
        alpha = jnp.exp(m_old - m_new)
        p = jnp.concatenate([jnp.exp(s[:, j * LANES:(j + 1) * LANES] - m_new)
                             for j in range(chunk // LANES)], axis=1)
        l_ref[...] = alpha * l_ref[...] + jnp.sum(p, axis=-1, keepdims=True)
        m_ref[...] = m_new
        acc_ref[...] = alpha * acc_ref[...] + jnp.dot(p.astype(BF16), v, preferred_element_type=F32)
        return carry

    lax.fori_loop(0, n_chunks, step, 0)
    o = acc_ref[...] * (1.0 / l_ref[...])
    for r in range(rep):
        o_ref[:, r * HEAD_DIM:(r + 1) * HEAD_DIM] = o[r * rows:(r + 1) * rows].astype(o_ref.dtype)


def _a_attn_prompt(q, k, v, bias, *, batch, seq):
    nqb = seq // Q_BLOCK
    rep = A_Q_HEADS // A_KV_HEADS
    n_chunks, chunk = bias.shape[1], bias.shape[3]
    qspec = pl.BlockSpec((Q_BLOCK, rep * HEAD_DIM), lambda b, h, q: (b * nqb + q, h))
    kvspec = pl.BlockSpec((seq, HEAD_DIM), lambda b, h, q: (b, h))
    stat = pltpu.VMEM((rep * Q_BLOCK, HEAD_DIM), F32)
    return pl.pallas_call(
        functools.partial(_a_attn_prompt_kernel, rep=rep),
        grid=(batch, A_KV_HEADS, nqb),
        in_specs=[qspec, kvspec, kvspec,
                  pl.BlockSpec((None, n_chunks, Q_BLOCK, chunk), lambda b, h, q: (b * nqb + q, 0, 0, 0))],
        out_specs=qspec,
        out_shape=jax.ShapeDtypeStruct((batch * seq, A_Q_HEADS * HEAD_DIM), BF16),
        scratch_shapes=[stat, stat, stat],
        compiler_params=_cparams(("arbitrary", "arbitrary", "arbitrary")), name="a_attn_prompt",
    )(q, k, v, bias)


def _a_score_sample_kernel(pt_ref, iq_ref, iw_ref, ikc_ref, ikn_ref, o_ref, *, n_pages):
    del pt_ref
    p = pl.program_id(1)
    ik = jnp.where(p == n_pages, ikn_ref[...], ikc_ref[...]).astype(BF16)
    iq = jnp.concatenate(
        [iq_ref[:, h * HEAD_DIM:(h + 1) * HEAD_DIM] for h in range(IDX_HEADS)], axis=0)
    rel = lax.dot_general(ik, iq.astype(BF16), (((1,), (1,)), ((), ())),
                          preferred_element_type=F32)
    w = jnp.maximum(rel, 0.0) * iw_ref[...]
    half = (IDX_HEADS * SUBLANES) // 2
    w = w[:, :half] + w[:, half:]
    shift = LANES // 2
    while shift >= SUBLANES:
        w = w + pltpu.roll(w, shift, 1)
        shift //= 2
    o_ref[...] = (w * IDX_SCALE).T[:SUBLANES, :]


def _a_score_sample(pt, iq_s, iw_row, cache_idx, ik_new, layer):
    nb, n_pages = pt.shape
    qrows = iq_s.shape[1]
    return pl.pallas_call(
        functools.partial(_a_score_sample_kernel, n_pages=n_pages),
        grid_spec=pltpu.PrefetchScalarGridSpec(
            num_scalar_prefetch=1, grid=(nb, n_pages + 1),
            in_specs=[pl.BlockSpec((None, qrows, IDX_HEADS * HEAD_DIM), lambda b, p, pt_r: (b, 0, 0)),
                      pl.BlockSpec((None, 1, IDX_HEADS * qrows), lambda b, p, pt_r: (b, 0, 0)),
                      pl.BlockSpec((None, None, PAGE_SIZE, HEAD_DIM),
                                   lambda b, p, pt_r: (layer, pt_r[b, jnp.minimum(p, n_pages - 1)], 0, 0)),
                      pl.BlockSpec((None, PAGE_SIZE, HEAD_DIM), lambda b, p, pt_r: (b, 0, 0))],
            out_specs=pl.BlockSpec((None, qrows, PAGE_SIZE), lambda b, p, pt_r: (b, 0, p))),
        out_shape=jax.ShapeDtypeStruct((nb, qrows, (n_pages + 1) * PAGE_SIZE), F32),
        compiler_params=_cparams(("arbitrary", "arbitrary")), name="a_score_sample",
    )(pt, iq_s, iw_row, cache_idx, ik_new)


def _a_select_sample_kernel(s_ref, o_ref, *, n_sel, pos0):
    q_pos = pos0 + lax.broadcasted_iota(I32, (s_ref.shape[0], 1), 0)
    o_ref[...] = _select_bias(s_ref[...], q_pos, n_sel)


def _a_select_sample(scores, *, n_sel, pos0):
    nb, qrows, nk = scores.shape
    spec = pl.BlockSpec((None, qrows, nk), lambda b: (b, 0, 0))
    return pl.pallas_call(
        functools.partial(_a_select_sample_kernel, n_sel=n_sel, pos0=pos0),
        grid=(nb,), in_specs=[spec], out_specs=spec,
        out_shape=jax.ShapeDtypeStruct(scores.shape, F32),
        compiler_params=_cparams(("arbitrary",)), name="a_select_sample",
    )(scores)


def _a_attn_sample_kernel(pt_ref, q_ref, bias_ref, kc_ref, vc_ref, kn_ref, vn_ref, o_ref,
                          m_ref, l_ref, acc_ref, *, n_pages, rep):
    del pt_ref
    p = pl.program_id(1)
    qrows = q_ref.shape[0]
    n_heads = A_KV_HEADS * rep
    n_cols = kc_ref.shape[0]

    @pl.when(p == 0)
    def _():
        m_ref[...] = jnp.full(m_ref.shape, -jnp.inf, F32)
        l_ref[...] = jnp.zeros(l_ref.shape, F32)
        acc_ref[...] = jnp.zeros(acc_ref.shape, F32)

    is_new = p == n_pages
    k = jnp.where(is_new, kn_ref[...], kc_ref[...]).astype(BF16)
    v = jnp.where(is_new, vn_ref[...], vc_ref[...]).astype(BF16)
    q = jnp.concatenate([q_ref[:, hd * HEAD_DIM:(hd + 1) * HEAD_DIM] for hd in range(n_heads)],
                        axis=0).astype(BF16)
    s = lax.dot_general(q, k, (((1,), (1,)), ((), ())), preferred_element_type=F32)
    kv_shift = A_KV_HEADS.bit_length() - 1
    row_shift = (rep * qrows).bit_length() - 1
    assert A_KV_HEADS == 1 << kv_shift and rep * qrows == 1 << row_shift
    spread = ((lax.broadcasted_iota(I32, (PAGE_SIZE, n_cols), 1) >> kv_shift) ==
              lax.broadcasted_iota(I32, (PAGE_SIZE, n_cols), 0))
    bias = jnp.dot(bias_ref[...].astype(BF16), jnp.where(spread, 1.0, 0.0).astype(BF16),
                   preferred_element_type=F32)
    s = s * ATTN_SCALE + jnp.concatenate([bias] * n_heads, axis=0)
    row_kv = lax.broadcasted_iota(I32, s.shape, 0) >> row_shift
    col_kv = lax.broadcasted_iota(I32, s.shape, 1) & (A_KV_HEADS - 1)
    s = jnp.where(row_kv == col_kv, s, MASK_BIAS)
    m_old = m_ref[...]
    m_new = jnp.maximum(m_old, jnp.max(s, axis=-1, keepdims=True))
    alpha = jnp.exp(m_old - m_new)
    pexp = jnp.concatenate([jnp.exp(s[:, j * LANES:(j + 1) * LANES] - m_new)
                            for j in range(n_cols // LANES)], axis=1)
    l_ref[...] = alpha * l_ref[...] + jnp.sum(pexp, axis=-1, keepdims=True)
    m_ref[...] = m_new
    acc_ref[...] = alpha * acc_ref[...] + jnp.dot(pexp.astype(BF16), v, preferred_element_type=F32)

    @pl.when(is_new)
    def _():
        o = acc_ref[...] * (1.0 / l_ref[...])
        for hd in range(n_heads):
            o_ref[:, hd * HEAD_DIM:(hd + 1) * HEAD_DIM] = o[hd * qrows:(hd + 1) * qrows]


def _a_attn_sample(pt, q_s, bias, cache_k, cache_v, k_new, v_new, layer):
    nb, n_pages = pt.shape
    qrows = q_s.shape[1]
    rep = A_Q_HEADS // A_KV_HEADS
    n_cols = PAGE_SIZE * A_KV_HEADS
    page_spec = pl.BlockSpec(
        (None, None, n_cols, HEAD_DIM),
        lambda b, p, pt_r: (layer, pt_r[b, jnp.minimum(p, n_pages - 1)], 0, 0))
    new_spec = pl.BlockSpec((None, n_cols, HEAD_DIM), lambda b, p, pt_r: (b, 0, 0))
    qspec = pl.BlockSpec((None, qrows, A_Q_HEADS * HEAD_DIM), lambda b, p, pt_r: (b, 0, 0))
    stat = pltpu.VMEM((A_Q_HEADS * qrows, HEAD_DIM), F32)
    return pl.pallas_call(
        functools.partial(_a_attn_sample_kernel, n_pages=n_pages, rep=rep),
        grid_spec=pltpu.PrefetchScalarGridSpec(
            num_scalar_prefetch=1, grid=(nb, n_pages + 1),
            in_specs=[qspec,
                      pl.BlockSpec((None, qrows, PAGE_SIZE), lambda b, p, pt_r: (b, 0, p)),
                      page_spec, page_spec, new_spec, new_spec],
            out_specs=qspec,
            scratch_shapes=[stat, stat, stat]),
        out_shape=jax.ShapeDtypeStruct(q_s.shape, F32),
        compiler_params=_cparams(("arbitrary", "arbitrary")), name="a_attn_sample",
    )(pt, q_s, bias, cache_k, cache_v, k_new, v_new)


def _dilated_bias(dist, window, dilation):
    z = dist if dilation == 1 else dist | lax.shift_left(dist & (dilation - 1), 16)
    return jnp.where(z >= 0, jnp.where(z <= window, 0.0, MASK_BIAS), MASK_BIAS)


def _dilated_merge(s_list, v_list, bias_list, rows, rep):
    m = None
    s_b = []
    for s, bias in zip(s_list, bias_list):
        sb = s * ATTN_SCALE + jnp.concatenate([bias] * rep, axis=0)
        s_b.append(sb)
        mg = jnp.max(sb, axis=-1, keepdims=True)
        m = mg if m is None else jnp.maximum(m, mg)
    l = jnp.zeros_like(m)
    o = None
    for sb, v in zip(s_b, v_list):
        p = jnp.exp(sb - m)
        l = l + jnp.sum(p, axis=-1, keepdims=True)
        pv = jnp.dot(p.astype(BF16), v, preferred_element_type=F32)
        o = pv if o is None else o + pv
    return o * (1.0 / l)


def _rows(start, size, stride):
    return pl.ds(start, size) if stride == 1 else pl.ds(start, size, stride=stride)


def _b_attn_prompt_kernel(q1_ref, q2_ref, q3_ref, k_ref, v_ref, o_ref,
                          qh_ref, kp_ref, vp_ref, og_ref, mg_ref, lg_ref, *, rep, dils):
    sb = pl.program_id(2)
    seq = k_ref.shape[0]
    sblk = q1_ref.shape[0]

    @pl.when(sb == 0)
    def _():
        for g, d in enumerate(dils):
            length = seq // d
            for r in range(d):
                kp_ref[g, r * length:(r + 1) * length, :] = k_ref[_rows(r, length, d), :].astype(BF16)
                vp_ref[g, r * length:(r + 1) * length, :] = v_ref[_rows(r, length, d), :].astype(BF16)

    for g, q_ref in enumerate((q1_ref, q2_ref, q3_ref)):
        for hd in range(rep):
            qh_ref[g, hd] = q_ref[:, hd * HEAD_DIM:(hd + 1) * HEAD_DIM]

    for g, d in enumerate(dils):
        length = seq // d
        per_step = sblk // d
        nj = min(B_TAPS, per_step)
        win = min(B_TAPS + nj, length)
        for r in range(d):
            for it in range(per_step // nj):
                loc = r + d * it * nj
                ja = sb * per_step + it * nj
                if win == length:
                    cstart = 0
                else:
                    cstart = pl.multiple_of(jnp.clip(ja - B_TAPS, 0, length - win), B_TAPS)
                qs = jnp.concatenate([qh_ref[g, hd, _rows(loc, nj, d), :] for hd in range(rep)],
                                     axis=0).astype(BF16)
                kw = kp_ref[g, pl.ds(r * length + cstart, win), :]
                vw = vp_ref[g, pl.ds(r * length + cstart, win), :]
                s = lax.dot_general(qs, kw, (((1,), (1,)), ((), ())), preferred_element_type=F32)
                dist = (ja + lax.broadcasted_iota(I32, (nj, win), 0)) - \
                       (cstart + lax.broadcasted_iota(I32, (nj, win), 1))
                bias = _dilated_bias(dist, B_TAPS, 1)
                s = s * ATTN_SCALE + jnp.concatenate([bias] * rep, axis=0)
                m = jnp.max(s, axis=-1, keepdims=True)
                p = jnp.exp(s - m)
                l = jnp.sum(p, axis=-1, keepdims=True)
                o = jnp.dot(p.astype(BF16), vw, preferred_element_type=F32)
                for hd in range(rep):
                    rows = _rows(loc, nj, d)
                    og_ref[g, hd, rows, :] = o[hd * nj:(hd + 1) * nj]
                    mg_ref[g, hd, rows, :] = jnp.broadcast_to(m[hd * nj:(hd + 1) * nj], (nj, HEAD_DIM))
                    lg_ref[g, hd, rows, :] = jnp.broadcast_to(l[hd * nj:(hd + 1) * nj], (nj, HEAD_DIM))

    n_groups = len(dils)
    for hd in range(rep):
        m = mg_ref[0, hd]
        for g in range(1, n_groups):
            m = jnp.maximum(m, mg_ref[g, hd])
        num = jnp.zeros((sblk, HEAD_DIM), F32)
        den = jnp.zeros((sblk, HEAD_DIM), F32)
        for g in range(n_groups):
            w = jnp.exp(mg_ref[g, hd] - m)
            num = num + w * og_ref[g, hd]
            den = den + w * lg_ref[g, hd]
        o_ref[:, hd * HEAD_DIM:(hd + 1) * HEAD_DIM] = (num * (1.0 / den)).astype(o_ref.dtype)


def _b_attn_prompt(q3, k, v, *, batch, seq):
    dils = tuple(d for _, d in DILATED_GROUPS)
    assert all(w == d * B_TAPS for w, d in DILATED_GROUPS) and seq % B_QUERY_BLOCK == 0
    assert all(B_QUERY_BLOCK % d == 0 and (seq // d) % B_TAPS == 0 for d in dils)
    nsb = seq // B_QUERY_BLOCK
    rep = B_Q_HEADS // B_KV_HEADS
    w = rep * HEAD_DIM
    n_groups = len(dils)

    def qspec(g):
        return pl.BlockSpec((B_QUERY_BLOCK, w), lambda b, h, s: (b * nsb + s, g * B_KV_HEADS + h))

    kvspec = pl.BlockSpec((seq, HEAD_DIM), lambda b, h, s: (b, h))
    part = pltpu.VMEM((n_groups, rep, B_QUERY_BLOCK, HEAD_DIM), F32)
    perm = pltpu.VMEM((n_groups, seq, HEAD_DIM), BF16)
    return pl.pallas_call(
        functools.partial(_b_attn_prompt_kernel, rep=rep, dils=dils),
        grid=(batch, B_KV_HEADS, nsb),
        in_specs=[qspec(0), qspec(1), qspec(2), kvspec, kvspec],
        out_specs=pl.BlockSpec((B_QUERY_BLOCK, w), lambda b, h, s: (b * nsb + s, h)),
        out_shape=jax.ShapeDtypeStruct((batch * seq, B_Q_HEADS * HEAD_DIM), BF16),
        scratch_shapes=[part, perm, perm, part, part, part],
        compiler_params=_cparams(("arbitrary", "arbitrary", "arbitrary")), name="b_attn_prompt",
    )(q3, q3, q3, k, v)


def _b_attn_sample_kernel(q1_ref, q2_ref, q3_ref, k_ref, v_ref, o_ref, *, rep, groups, q_idx0):
    rows = q1_ref.shape[0]
    n_ctx = k_ref.shape[0]
    k = k_ref[...].astype(BF16)
    v = v_ref[...].astype(BF16)
    dist = (q_idx0 + lax.broadcasted_iota(I32, (rows, n_ctx), 0)) - \
        lax.broadcasted_iota(I32, (rows, n_ctx), 1)
    s_list, bias_list = [], []
    for q_ref, (window, dilation) in zip((q1_ref, q2_ref, q3_ref), groups):
        s_list.append(lax.dot_general(_stack_heads(q_ref, rep), k, (((1,), (1,)), ((), ())),
                                      preferred_element_type=F32))
        bias_list.append(_dilated_bias(dist, window, dilation))
    o = _dilated_merge(s_list, [v] * len(groups), bias_list, rows, rep)
    for r in range(rep):
        o_ref[:, r * HEAD_DIM:(r + 1) * HEAD_DIM] = o[r * rows:(r + 1) * rows]


def _b_attn_sample(q3_s, ctx_k, ctx_v, *, q_idx0):
    nb, qrows, _ = q3_s.shape
    n_ctx = ctx_k.shape[1]
    rep = B_Q_HEADS // B_KV_HEADS
    w = rep * HEAD_DIM

    def qspec(g):
        return pl.BlockSpec((None, qrows, w), lambda b, h: (b, 0, g * B_KV_HEADS + h))

    kvspec = pl.BlockSpec((None, n_ctx, HEAD_DIM), lambda b, h: (b, 0, h))
    return pl.pallas_call(
        functools.partial(_b_attn_sample_kernel, rep=rep, groups=DILATED_GROUPS, q_idx0=q_idx0),
        grid=(nb, B_KV_HEADS),
        in_specs=[qspec(0), qspec(1), qspec(2), kvspec, kvspec],
        out_specs=pl.BlockSpec((None, qrows, w), lambda b, h: (b, 0, h)),
        out_shape=jax.ShapeDtypeStruct((nb, qrows, B_Q_HEADS * HEAD_DIM), F32),
        compiler_params=_cparams(("arbitrary", "arbitrary")), name="b_attn_sample",
    )(q3_s, q3_s, q3_s, ctx_k, ctx_v)


def _router_kernel(x_ref, w_ref, o_ref):
    logits = jnp.dot(x_ref[...], w_ref[...], preferred_element_type=F32,
                     precision=lax.Precision.HIGHEST)
    lane = lax.broadcasted_iota(I32, logits.shape, 1).astype(F32)
    lg = jnp.where(lane < N_EXPERTS, logits, -jnp.inf)
    m1 = jnp.max(lg, axis=-1, keepdims=True)
    i1 = jnp.min(jnp.where(lg == m1, lane, float(LANES)), axis=-1, keepdims=True)
    lg2 = jnp.where(lane == i1, -jnp.inf, lg)
    m2 = jnp.max(lg2, axis=-1, keepdims=True)
    i2 = jnp.min(jnp.where(lg2 == m2, lane, float(LANES)), axis=-1, keepdims=True)
    e = jnp.exp(m2 - m1)
    g1 = 1.0 / (1.0 + e)
    g2 = e * g1
    o_ref[...] = jnp.where(lane == 0, g1, jnp.where(lane == 1, g2, jnp.where(
        lane == 2, i1, jnp.where(lane == 3, i2, 0.0))))


def _router(h, w_router_pad, layer):
    m, d = h.shape
    return pl.pallas_call(
        _router_kernel, grid=(m // TM_ROW,),
        in_specs=[pl.BlockSpec((TM_ROW, d), lambda i: (i, 0)),
                  pl.BlockSpec((None, d, LANES), lambda i: (layer, 0, 0))],
        out_specs=pl.BlockSpec((TM_ROW, LANES), lambda i: (i, 0)),
        out_shape=jax.ShapeDtypeStruct((m, LANES), F32),
        compiler_params=_cparams(("arbitrary",)), name="router",
    )(h, w_router_pad)


def _row_copy(src_hbm, row, dst, dst_row, sem):
    return pltpu.make_async_copy(src_hbm.at[pl.ds(row, 1)], dst.at[pl.ds(dst_row, 1)], sem)


def _dispatch_kernel(live_ref, tok_ref, h_hbm, o_ref, buf, sem):
    i = pl.program_id(0)
    rows = buf.shape[0]

    @pl.when(live_ref[i] > 0)
    def _():
        def start(r, c):
            _row_copy(h_hbm, tok_ref[0, r], buf, r, sem).start()
            return c

        def wait(r, c):
            _row_copy(h_hbm, 0, buf, r, sem).wait()
            return c

        lax.fori_loop(0, rows, start, 0)
        lax.fori_loop(0, rows, wait, 0)
        o_ref[...] = buf[...].astype(o_ref.dtype)

    @pl.when(live_ref[i] == 0)
    def _():
        o_ref[...] = jnp.zeros(o_ref.shape, o_ref.dtype)


def _dispatch(h, tok_of, tile_live):
    d = h.shape[1]
    n_rows = tok_of.shape[0]
    return pl.pallas_call(
        _dispatch_kernel,
        grid_spec=pltpu.PrefetchScalarGridSpec(
            num_scalar_prefetch=1, grid=(n_rows // TM_ROW,),
            in_specs=[pl.BlockSpec((None, 1, TM_ROW), lambda i, live: (i, 0, 0),
                                   memory_space=pltpu.SMEM),
                      pl.BlockSpec(memory_space=pl.ANY)],
            out_specs=pl.BlockSpec((TM_ROW, d), lambda i, live: (i, 0)),
            scratch_shapes=[pltpu.VMEM((TM_ROW, d), F32), pltpu.SemaphoreType.DMA(())]),
        out_shape=jax.ShapeDtypeStruct((n_rows, d), BF16),
        compiler_params=_cparams(("arbitrary",)), name="moe_dispatch",
    )(tile_live, tok_of.reshape(n_rows // TM_ROW, 1, TM_ROW), h)


def _combine_kernel(pos_ref, y_hbm, h_ref, gate_ref, g_ref, b_ref, o_ref, ob_ref, buf, sem):
    rows = h_ref.shape[0]

    def start(r, c):
        for kk in range(2):
            _row_copy(y_hbm, pos_ref[0, 2 * r + kk], buf.at[kk], r, sem).start()
        return c

    def wait(r, c):
        for kk in range(2):
            _row_copy(y_hbm, 0, buf.at[kk], r, sem).wait()
        return c

    lax.fori_loop(0, rows, start, 0)
    lax.fori_loop(0, rows, wait, 0)
    gates = gate_ref[...]
    ffn = gates[:, 0:1] * buf[0] + gates[:, 1:2] * buf[1]
    o = _layer_norm_rows(DEEPNORM_ALPHA * h_ref[...] + ffn, g_ref[...], b_ref[...])
    o_ref[...] = o
    ob_ref[...] = o.astype(BF16)


def _combine_deepnorm(h, y_sorted, pos, gates, g, b):
    m, d = h.shape
    row = pl.BlockSpec((TM_ROW, d), lambda i: (i, 0))
    vec = pl.BlockSpec((1, d), lambda i: (0, 0))
    return pl.pallas_call(
        _combine_kernel,
        grid=(m // TM_ROW,),
        in_specs=[pl.BlockSpec((None, 1, 2 * TM_ROW), lambda i: (i, 0, 0), memory_space=pltpu.SMEM),
                  pl.BlockSpec(memory_space=pl.ANY), row,
                  pl.BlockSpec((TM_ROW, LANES), lambda i: (i, 0)), vec, vec],
        out_specs=[row, row],
        scratch_shapes=[pltpu.VMEM((2, TM_ROW, d), F32), pltpu.SemaphoreType.DMA(())],
        out_shape=[jax.ShapeDtypeStruct((m, d), F32), jax.ShapeDtypeStruct((m, d), BF16)],
        compiler_params=_cparams(("arbitrary",)), name="moe_combine",
    )(pos.reshape(m // TM_ROW, 1, 2 * TM_ROW), y_sorted, h, gates, g.reshape(1, d), b.reshape(1, d))


def _moe_plan(top_idx, n_tiles_max):
    n_assign = top_idx.size
    e_flat = top_idx.reshape(-1)
    onehot = (e_flat[:, None] == jnp.arange(N_EXPERTS, dtype=I32)[None, :]).astype(I32)
    csum = jnp.cumsum(onehot, axis=0)
    counts = csum[-1]
    rank = jnp.sum((csum - onehot) * onehot, axis=1)
    tiles_e = (counts + TM_MOE - 1) // TM_MOE
    tile_end = jnp.cumsum(tiles_e)
    tile_start = tile_end - tiles_e
    pos = jnp.sum(onehot * tile_start[None, :], axis=1) * TM_MOE + rank
    n_rows = n_tiles_max * TM_MOE
    tok_of = jnp.zeros((n_rows,), I32).at[pos].set(jnp.arange(n_assign, dtype=I32) // 2)
    n_used = tile_end[-1]
    t = jnp.arange(n_tiles_max, dtype=I32)
    gid = jnp.minimum(jnp.sum((t[:, None] >= tile_end[None, :]).astype(I32), axis=1), N_EXPERTS - 1)
    nv = jnp.clip(counts[gid] - (t - tile_start[gid]) * TM_MOE, 0, TM_MOE)
    nv = jnp.where(t < n_used, nv, 0).astype(I32)
    gid = jnp.where(t < n_used, gid, gid[jnp.maximum(n_used - 1, 0)])
    sub = jnp.arange(n_rows // TM_ROW, dtype=I32)
    live = (nv[sub * TM_ROW // TM_MOE] > (sub * TM_ROW) % TM_MOE).astype(I32)
    return pos, tok_of, gid, nv, n_used.astype(I32), live


def _rope_tables(pos):
    half = HEAD_DIM // 2
    inv_freq = jnp.power(ROPE_THETA, -jnp.arange(half, dtype=F32) / half)
    ang = pos.astype(F32)[:, None] * inv_freq[None, :]
    cos, sin = jnp.cos(ang), jnp.sin(ang)
    return jnp.concatenate([cos, cos], axis=1), jnp.concatenate([-sin, sin], axis=1)


def _sample_rows(a, n_prompt, nb, nq, qrows):
    x = a[n_prompt:n_prompt + nb * nq].reshape(nb, nq, a.shape[1]).astype(F32)
    return jnp.pad(x, ((0, 0), (0, qrows - nq), (0, 0)))


def _as_page(rows_s, nb, nq):
    x = rows_s.reshape(nb, nq, rows_s.shape[1])
    return jnp.pad(x, ((0, 0), (0, PAGE_SIZE - nq), (0, 0)))


def kernel(x_prompt, x_sample, cache_a_k, cache_a_v, cache_a_idx, state_b_k, state_b_v, page_table,
           w_a_in, w_a_out, w_kv_shared, w_b_q, w_b_out, ln_mix_g, ln_mix_b, ln_ffn_g, ln_ffn_b,
           w_ffn_in, w_ffn_down, w_router, w_exp_in, w_exp_down):
    batch, seq, d = x_prompt.shape
    nb, nq, _ = x_sample.shape
    n_prompt, n_samp = batch * seq, nb * nq
    past_len = page_table.shape[1] * PAGE_SIZE
    n_a_layers = w_a_in.shape[0]
    m_pad = pl.cdiv(n_prompt + n_samp, TM_DENSE) * TM_DENSE
    assert m_pad % TM_ROW == 0 and n_prompt % TM_ROW == 0
    qrows = SUBLANES
    n_sel_p = min(IDX_TOPK_MAX, seq // 4)
    n_sel_s = min(IDX_TOPK_MAX, (past_len + nq) // 4)
    w_buf = state_b_k.shape[1]

    pos_all = jnp.concatenate([jnp.tile(jnp.arange(seq, dtype=I32), batch),
                               jnp.tile(past_len + jnp.arange(nq, dtype=I32), nb),
                               jnp.zeros((m_pad - n_prompt - n_samp,), I32)])
    tabs = _rope_tables(pos_all)

    h = jnp.concatenate([x_prompt.reshape(n_prompt, d), x_sample.reshape(n_samp, d),
                         jnp.zeros((m_pad - n_prompt - n_samp, d), F32)], axis=0)
    hb = h.astype(BF16)

    qw = A_Q_HEADS * HEAD_DIM
    kvw = A_KV_HEADS * HEAD_DIM
    iqw = IDX_HEADS * HEAD_DIM
    cache_k_rows = cache_a_k.reshape(cache_a_k.shape[:2] + (-1, HEAD_DIM))
    cache_v_rows = cache_a_v.reshape(cache_a_v.shape[:2] + (-1, HEAD_DIM))
    w_router_pad = jnp.pad(w_router, ((0, 0), (0, 0), (0, LANES - N_EXPERTS)))
    w_exp_in2 = w_exp_in.reshape((-1,) + w_exp_in.shape[2:])
    w_exp_down2 = w_exp_down.reshape((-1,) + w_exp_down.shape[2:])
    w_kv3 = w_kv_shared[None]
    w_iw = w_a_in[:, :, qw + 2 * kvw + iqw + HEAD_DIM:]
    n_tiles_moe = (2 * m_pad) // TM_MOE + N_EXPERTS

    def pad_rows(mix_p, mix_s):
        return jnp.concatenate([mix_p, mix_s[:, :nq].reshape(n_samp, -1).astype(BF16),
                                jnp.zeros((m_pad - n_prompt - n_samp, mix_p.shape[1]), BF16)], axis=0)

    def heads(a, lead):
        return a.reshape(lead + (A_KV_HEADS, HEAD_DIM))

    ak, av, ai = [], [], []
    kb = vb = ctx_k = ctx_v = None
    for l in range(DEPTH):
        if l < n_a_layers:
            proj = functools.partial(_dense, hb, w_a_in, l, tm=TM_DENSE)
            q = proj(col0=0, n_out=qw, tn=TN_PROJ, mode="rope", tabs=tabs, name="a_q")
            k = proj(col0=qw, n_out=kvw, tn=TN_PROJ, mode="rope", tabs=tabs, name="a_k")
            v = proj(col0=qw + kvw, n_out=kvw, tn=TN_PROJ, name="a_v")
            iq = proj(col0=qw + 2 * kvw, n_out=iqw, tn=TN_PROJ, mode="rope", tabs=tabs, name="a_iq")
            ik = proj(col0=qw + 2 * kvw + iqw, n_out=HEAD_DIM, tn=LANES, mode="rope", tabs=tabs,
                      name="a_ik")
            iw = _dense(hb, w_iw, l, col0=0, n_out=IDX_HEADS, tm=TM_DENSE, tn=IDX_HEADS, name="a_iw")
            ak.append(k)
            av.append(v)
            ai.append(ik)
            bias_p = _a_select_prompt(iq, ik, iw.T, batch=batch, seq=seq, n_sel=n_sel_p)
            mix_p = _a_attn_prompt(q, k, v, bias_p, batch=batch, seq=seq)
            iq_s = _sample_rows(iq, n_prompt, nb, nq, qrows)
            iw_s = _sample_rows(iw, n_prompt, nb, nq, qrows)
            iw_row = jnp.transpose(iw_s, (0, 2, 1)).reshape(nb, 1, IDX_HEADS * qrows)
            scores = _a_score_sample(page_table, iq_s, iw_row, cache_a_idx,
                                     _as_page(ik[n_prompt:n_prompt + n_samp], nb, nq), l)
            bias_s = _a_select_sample(scores, n_sel=n_sel_s, pos0=past_len)
            mix_s = _a_attn_sample(page_table, _sample_rows(q, n_prompt, nb, nq, qrows), bias_s,
                                   cache_k_rows, cache_v_rows,
                                   _as_page(k[n_prompt:n_prompt + n_samp], nb, nq).reshape(nb, -1, HEAD_DIM),
                                   _as_page(v[n_prompt:n_prompt + n_samp], nb, nq).reshape(nb, -1, HEAD_DIM), l)
            w_out, lo = w_a_out, l
        else:
            j = l - n_a_layers
            q3 = _dense(hb, w_b_q, j, col0=0, n_out=w_b_q.shape[2], tm=TM_DENSE, tn=TN_PROJ,
                        mode="rope", tabs=tabs, name="b_q")
            mix_p = _b_attn_prompt(q3, kb, vb, batch=batch, seq=seq)
            mix_s = _b_attn_sample(_sample_rows(q3, n_prompt, nb, nq, qrows), ctx_k, ctx_v,
                                   q_idx0=w_buf)
            w_out, lo = w_b_out, j
        sub = _dense(pad_rows(mix_p, mix_s), w_out, lo, col0=0, n_out=d, tm=TM_DENSE, tn=TN_PROJ,
                     name="mix_out")
        h, hb = _deepnorm(h, sub, ln_mix_g[l], ln_mix_b[l])

        f = l // 2
        if l % 2 == 0:
            act = _dense(hb, w_ffn_in, f, col0=0, n_out=D_FF, up_col0=D_FF, tm=TM_DENSE, tn=TN_FFN,
                         mode="swiglu", out_dtype=BF16, name="ffn_in")
            sub = _dense(act, w_ffn_down, f, col0=0, n_out=d, tm=TM_DENSE, tn=TN_DOWN, tk=TK_DOWN,
                         name="ffn_down")
            h, hb = _deepnorm(h, sub, ln_ffn_g[l], ln_ffn_b[l])
        else:
            route = _router(h, w_router_pad, f)
            gates = route
            top_idx = route[:, 2:4].astype(I32)
            pos, tok_of, gid, nv, n_used, live = _moe_plan(top_idx, n_tiles_moe)
            gid = gid + f * N_EXPERTS
            xs = _dispatch(h, tok_of, live)
            act = _matmul(xs, w_exp_in2, gid, nv, n_used, col0=0, n_out=D_FF, up_col0=D_FF,
                          tm=TM_MOE, tn=TN_FFN, mode="swiglu", out_dtype=BF16, n_paths=MOE_PATHS,
                          name="moe_in")
            ys = _matmul(act, w_exp_down2, gid, nv, n_used, col0=0, n_out=d, tm=TM_MOE, tn=TN_DOWN,
                         tk=TK_DOWN, n_paths=MOE_PATHS, name="moe_down")
            h, hb = _combine_deepnorm(h, ys, pos, gates, ln_ffn_g[l], ln_ffn_b[l])

        if l == n_a_layers - 1:
            kb = _dense(hb, w_kv3, 0, col0=0, n_out=kvw, tm=TM_DENSE, tn=TN_PROJ, mode="rope",
                        tabs=tabs, name="kv_k")
            vb = _dense(hb, w_kv3, 0, col0=kvw, n_out=kvw, tm=TM_DENSE, tn=TN_PROJ, name="kv_v")
            kb_new = kb[n_prompt:n_prompt + n_samp].reshape(nb, nq, kvw)
            vb_new = vb[n_prompt:n_prompt + n_samp].reshape(nb, nq, kvw)
            ctx_pad = (-(w_buf + nq)) % LANES
            ctx_k = jnp.concatenate([state_b_k.reshape(nb, w_buf, kvw), kb_new,
                                     jnp.zeros((nb, ctx_pad, kvw), F32)], axis=1)
            ctx_v = jnp.concatenate([state_b_v.reshape(nb, w_buf, kvw), vb_new,
                                     jnp.zeros((nb, ctx_pad, kvw), F32)], axis=1)

    y_prompt = h[:n_prompt].reshape(batch, seq, d)
    y_sample = h[n_prompt:n_prompt + n_samp].reshape(nb, nq, d)
    new_a_k_prompt = jnp.stack([heads(a[:n_prompt], (batch, seq)) for a in ak])
    new_a_v_prompt = jnp.stack([heads(a[:n_prompt], (batch, seq)) for a in av])
    new_a_idx_prompt = jnp.stack([a[:n_prompt].reshape(batch, seq, HEAD_DIM) for a in ai])
    new_a_k_sample = jnp.stack([heads(a[n_prompt:n_prompt + n_samp], (nb, nq)) for a in ak])
    new_a_v_sample = jnp.stack([heads(a[n_prompt:n_prompt + n_samp], (nb, nq)) for a in av])
    new_a_idx_sample = jnp.stack([a[n_prompt:n_prompt + n_samp].reshape(nb, nq, HEAD_DIM) for a in ai])
    w_keep = min(2048, seq)
    new_b_k_prompt = heads(kb[:n_prompt], (batch, seq))[:, seq - w_keep:]
    new_b_v_prompt = heads(vb[:n_prompt], (batch, seq))[:, seq - w_keep:]
    new_b_k_sample = heads(ctx_k[:, nq:nq + w_buf], (nb, w_buf))
    new_b_v_sample = heads(ctx_v[:, nq:nq + w_buf], (nb, w_buf))
    return (y_prompt, y_sample, new_a_k_prompt, new_a_v_prompt, new_a_idx_prompt, new_a_k_sample,
            new_a_v_sample, new_a_idx_sample, new_b_k_prompt, new_b_v_prompt, new_b_k_sample,
            new_b_v_sample)
```

```python
import functools
import math

import jax
import jax.numpy as jnp
from jax import lax
from jax.experimental import pallas as pl
from jax.experimental.pallas import tpu as pltpu

F32 = jnp.float32
BF16 = jnp.bfloat16
I32 = jnp.int32

LANES = 128
SUBLANES = 8
VMEM_LIMIT_BYTES = 56 * 1024 * 1024

D_MODEL = 4096
HEAD_DIM = 128
PAGE_SIZE = 128
A_Q_HEADS = 32
A_KV_HEADS = 8
IDX_HEADS = 32
IDX_TOPK_MAX = 256
B_Q_HEADS = 32
B_KV_HEADS = 8
DILATED_GROUPS = ((128, 1), (512, 4), (2048, 16))
N_EXPERTS = 8
D_FF = 14336
ROPE_THETA = 10000.0
Q_BLOCK = 128
LN_EPS = 1e-5
DEPTH = 4
DEEPNORM_ALPHA = (2.0 * DEPTH) ** 0.25
ATTN_SCALE = HEAD_DIM ** -0.5
IDX_SCALE = (IDX_HEADS ** -0.5) * (HEAD_DIM ** -0.5)
MASK_BIAS = -1e30
INT_MIN = -(2 ** 31)

TM_DENSE = 1664
TM_MOE = 2176
TM_ROW = 128
TN_PROJ = 512
TN_FFN = 256
TN_DOWN = 1024
TK_DOWN = 512
MOE_PATHS = 8
KEY_CHUNK = 512
B_TAPS = 128
B_QUERY_BLOCK = 512


def _cparams(sem):
    return pltpu.CompilerParams(dimension_semantics=sem, vmem_limit_bytes=VMEM_LIMIT_BYTES)


def _rope_tile(y, cos, sin_signed):
    outs = []
    for g in range(y.shape[1] // HEAD_DIM):
        yh = y[:, g * HEAD_DIM:(g + 1) * HEAD_DIM]
        outs.append(yh * cos + pltpu.roll(yh, HEAD_DIM // 2, 1) * sin_signed)
    return outs[0] if len(outs) == 1 else jnp.concatenate(outs, axis=1)


def _mm_rows(refs, rows, mode, nk, k):
    o_ref = refs[-1]
    tm = o_ref.shape[0]
    if mode == "swiglu":
        x_ref, wg_ref, wu_ref, _ = refs
        x = x_ref[0:rows, :]
        g = jnp.dot(x, wg_ref[...].astype(BF16), preferred_element_type=F32)
        u = jnp.dot(x, wu_ref[...].astype(BF16), preferred_element_type=F32)
        o_ref[0:rows, :] = (g * (1.0 / (1.0 + jnp.exp(-g))) * u).astype(o_ref.dtype)
    elif mode == "rope":
        x_ref, w_ref, cos_ref, sin_ref, _ = refs
        y = jnp.dot(x_ref[0:rows, :], w_ref[...].astype(BF16), preferred_element_type=F32)
        o_ref[0:rows, :] = _rope_tile(y, cos_ref[0:rows, :], sin_ref[0:rows, :]).astype(o_ref.dtype)
    else:
        x_ref, w_ref, _ = refs

        def dot():
            return jnp.dot(x_ref[0:rows, :], w_ref[...].astype(BF16), preferred_element_type=F32)

        if nk == 1:
            o_ref[0:rows, :] = dot().astype(o_ref.dtype)
        else:
            @pl.when(k == 0)
            def _():
                o_ref[0:rows, :] = dot()

            @pl.when(k > 0)
            def _():
                o_ref[0:rows, :] += dot()
    if rows < tm:
        o_ref[rows:tm, :] = jnp.zeros((tm - rows, o_ref.shape[1]), o_ref.dtype)


def _mm_kernel(meta_ref, gid_ref, nv_ref, *refs, mode, nk, n_paths):
    del meta_ref, gid_ref
    o_ref = refs[-1]
    k = pl.program_id(2)
    step = o_ref.shape[0] // n_paths
    n_sub = (nv_ref[pl.program_id(0)] + step - 1) // step
    for s in range(1, n_paths + 1):
        pl.when(n_sub == s)(functools.partial(_mm_rows, refs, s * step, mode, nk, k))

    @pl.when(n_sub == 0)
    def _():
        o_ref[...] = jnp.zeros(o_ref.shape, o_ref.dtype)


def _matmul(x, w, gid, nv, n_used, *, col0, n_out, tm, tn, tk=None, mode="plain",
            tabs=None, out_dtype=F32, up_col0=None, n_paths=1, name="mm"):
    m, kdim = x.shape
    tk = kdim if tk is None else tk
    nk = kdim // tk
    assert m % tm == 0 and kdim % tk == 0 and col0 % tn == 0
    assert tm % (n_paths * 2 * SUBLANES) == 0
    assert mode == "plain" or nk == 1
    assert out_dtype == F32 or nk == 1
    cb0 = col0 // tn
    grid = (m // tm, pl.cdiv(n_out, tn), nk)

    def x_map(i, j, k, meta, gid_r, nv_r):
        return (jnp.minimum(i, meta[0] - 1), k)

    def w_map_at(cb):
        def w_map(i, j, k, meta, gid_r, nv_r):
            live = nv_r[i] > 0
            return (gid_r[i], jnp.where(live, k, 0), cb + jnp.where(live, j, 0))
        return w_map

    def row_map(i, j, k, meta, gid_r, nv_r):
        return (i, 0)

    def o_map(i, j, k, meta, gid_r, nv_r):
        return (i, j)

    x_mode = pl.Buffered(1) if nk == 1 else None
    in_specs = [pl.BlockSpec((tm, tk), x_map, pipeline_mode=x_mode),
                pl.BlockSpec((None, tk, tn), w_map_at(cb0))]
    args = [x, w]
    if mode == "swiglu":
        assert up_col0 % tn == 0
        in_specs.append(pl.BlockSpec((None, tk, tn), w_map_at(up_col0 // tn)))
        args.append(w)
    if mode == "rope":
        in_specs += [pl.BlockSpec((tm, HEAD_DIM), row_map)] * 2
        args += list(tabs)
    meta = jnp.reshape(n_used, (1,)).astype(I32)
    return pl.pallas_call(
        functools.partial(_mm_kernel, mode=mode, nk=nk, n_paths=n_paths),
        grid_spec=pltpu.PrefetchScalarGridSpec(
            num_scalar_prefetch=3, grid=grid, in_specs=in_specs,
            out_specs=pl.BlockSpec((tm, tn), o_map)),
        out_shape=jax.ShapeDtypeStruct((m, n_out), out_dtype),
        compiler_params=_cparams(("arbitrary", "arbitrary", "arbitrary")),
        name=name,
    )(meta, gid, nv, *args)


def _dense(x, w, layer, **kw):
    nt = x.shape[0] // kw["tm"]
    gid = jnp.full((nt,), layer, I32)
    nv = jnp.full((nt,), kw["tm"], I32)
    return _matmul(x, w, gid, nv, jnp.asarray(nt, I32), **kw)


def _layer_norm_rows(y, g, b):
    mu = jnp.mean(y, axis=-1, keepdims=True)
    d = y - mu
    var = jnp.mean(d * d, axis=-1, keepdims=True)
    return d * lax.rsqrt(var + LN_EPS) * g + b


def _deepnorm_kernel(h_ref, s_ref, g_ref, b_ref, o_ref, ob_ref):
    y = DEEPNORM_ALPHA * h_ref[...] + s_ref[...]
    o = _layer_norm_rows(y, g_ref[...], b_ref[...])
    o_ref[...] = o
    ob_ref[...] = o.astype(BF16)


def _deepnorm(h, sub, g, b):
    m, d = h.shape
    row = pl.BlockSpec((TM_ROW, d), lambda i: (i, 0))
    vec = pl.BlockSpec((1, d), lambda i: (0, 0))
    return pl.pallas_call(
        _deepnorm_kernel, grid=(m // TM_ROW,),
        in_specs=[row, row, vec, vec], out_specs=[row, row],
        out_shape=[jax.ShapeDtypeStruct((m, d), F32), jax.ShapeDtypeStruct((m, d), BF16)],
        compiler_params=_cparams(("arbitrary",)), name="deepnorm",
    )(h, sub, g.reshape(1, d), b.reshape(1, d))


def _row_count(mask):
    return jnp.sum(jnp.where(mask, 1.0, 0.0), axis=-1, keepdims=True)


def _select_bias(score, q_pos, n_sel):
    r_rows, n_keys = score.shape
    idx = lax.broadcasted_iota(I32, (r_rows, n_keys), 1)
    adm = idx <= q_pos
    sc = jnp.where(adm, score + 0.0, -jnp.inf)
    bits = pltpu.bitcast(sc, I32)
    key = bits ^ ((bits >> 31) & 0x7FFFFFFF)
    kf = float(n_sel)

    t0 = jnp.where(_row_count(key >= 0) >= kf, 0, INT_MIN).astype(I32)

    def value_step(i, t):
        cand = t | lax.shift_left(jnp.int32(1), jnp.int32(30) - i)
        return jnp.where(_row_count(key >= cand) >= kf, cand, t)

    thr = lax.fori_loop(0, 31, value_step, t0)
    gt = key > thr
    need = kf - _row_count(gt)
    not_tie = jnp.int32(2 ** 30)
    tie_idx = jnp.where(key == thr, idx, not_tie)
    n_bits = n_keys.bit_length()

    def index_step(i, j0):
        cand = j0 + lax.shift_left(jnp.int32(1), jnp.int32(n_bits - 1) - i)
        return jnp.where(_row_count(tie_idx < cand) < need, cand, j0)

    def lowest_ties():
        return lax.fori_loop(0, n_bits, index_step, jnp.zeros((r_rows, 1), I32))

    surplus = jnp.max(_row_count(key >= thr)) > kf
    j0 = lax.cond(surplus, lowest_ties, lambda: jnp.full((r_rows, 1), not_tie - 1, I32))
    picked = jnp.where(gt, 0.0, jnp.where(tie_idx <= j0, 0.0, MASK_BIAS))
    return jnp.where(adm, picked, MASK_BIAS)


def _a_select_prompt_kernel(iq_ref, ik_ref, iwt_ref, bias_ref, *, n_sel):
    qb = pl.program_id(1)
    ik = ik_ref[...].astype(BF16)
    acc = jnp.zeros((ik.shape[0], Q_BLOCK), F32)
    for h in range(IDX_HEADS):
        iqh = iq_ref[:, h * HEAD_DIM:(h + 1) * HEAD_DIM].astype(BF16)
        rel = lax.dot_general(ik, iqh, (((1,), (1,)), ((), ())), preferred_element_type=F32)
        acc = acc + jnp.maximum(rel, 0.0) * iwt_ref[h:h + 1, :]
    score = (acc * IDX_SCALE).T
    q_pos = qb * Q_BLOCK + lax.broadcasted_iota(I32, (Q_BLOCK, 1), 0)
    bias = _select_bias(score, q_pos, n_sel).astype(bias_ref.dtype)
    chunk = bias_ref.shape[2]
    for c in range(bias_ref.shape[0]):
        bias_ref[c] = bias[:, c * chunk:(c + 1) * chunk]


def _a_select_prompt(iq, ik, iwt, *, batch, seq, n_sel):
    nqb = seq // Q_BLOCK
    chunk = min(KEY_CHUNK, seq)
    return pl.pallas_call(
        functools.partial(_a_select_prompt_kernel, n_sel=n_sel),
        grid=(batch, nqb),
        in_specs=[pl.BlockSpec((Q_BLOCK, IDX_HEADS * HEAD_DIM), lambda b, q: (b * nqb + q, 0)),
                  pl.BlockSpec((seq, HEAD_DIM), lambda b, q: (b, 0)),
                  pl.BlockSpec((IDX_HEADS, Q_BLOCK), lambda b, q: (0, b * nqb + q))],
        out_specs=pl.BlockSpec((None, seq // chunk, Q_BLOCK, chunk), lambda b, q: (b * nqb + q, 0, 0, 0)),
        out_shape=jax.ShapeDtypeStruct((batch * nqb, seq // chunk, Q_BLOCK, chunk), BF16),
        compiler_params=_cparams(("arbitrary", "arbitrary")), name="a_select_prompt",
    )(iq, ik, iwt)


def _stack_heads(q_ref, rep):
    return jnp.concatenate(
        [q_ref[:, r * HEAD_DIM:(r + 1) * HEAD_DIM] for r in range(rep)], axis=0).astype(BF16)


def _a_attn_prompt_kernel(q_ref, k_ref, v_ref, bias_ref, o_ref, m_ref, l_ref, acc_ref, *, rep):
    rows = q_ref.shape[0]
    chunk = bias_ref.shape[2]
    n_chunks = (pl.program_id(2) * rows + rows + chunk - 1) // chunk
    qs = _stack_heads(q_ref, rep)
    m_ref[...] = jnp.full(m_ref.shape, -jnp.inf, F32)
    l_ref[...] = jnp.zeros(l_ref.shape, F32)
    acc_ref[...] = jnp.zeros(acc_ref.shape, F32)

    def step(c, carry):
        start = pl.multiple_of(c * chunk, chunk)
        k = k_ref[pl.ds(start, chunk), :].astype(BF16)
        v = v_ref[pl.ds(start, chunk), :].astype(BF16)
        bias = bias_ref[c].astype(F32)
        s = lax.dot_general(qs, k, (((1,), (1,)), ((), ())), preferred_element_type=F32)
        s = s * ATTN_SCALE + jnp.concatenate([bias] * rep, axis=0)
        m_old = m_ref[...]
        m_new = jnp.maximum(m_old, jnp.max(s, axis=-1, keepdims=True))
        alpha = jnp.exp(m_old - m_new)
        p = jnp.concatenate([jnp.exp(s[:, j * LANES:(j + 1) * LANES] - m_new)
                             for j in range(chunk // LANES)], axis=1)
        l_ref[...] = alpha * l_ref[...] + jnp.sum(p, axis=-1, keepdims=True)
        m_ref[...] = m_new
        acc_ref[...] = alpha * acc_ref[...] + jnp.dot(p.astype(BF16), v, preferred_element_type=F32)
        return carry

    lax.fori_loop(0, n_chunks, step, 0)
    o = acc_ref[...] * (1.0 / l_ref[...])
    for r in range(rep):
        o_ref[:, r * HEAD_DIM:(r + 1) * HEAD_DIM] = o[r * rows:(r + 1) * rows].astype(o_ref.dtype)


def _a_attn_prompt(q, k, v, bias, *, batch, seq):
    nqb = seq // Q_BLOCK
    rep = A_Q_HEADS // A_KV_HEADS
    n_chunks, chunk = bias.shape[1], bias.shape[3]
    qspec = pl.BlockSpec((Q_BLOCK, rep * HEAD_DIM), lambda b, h, q: (b * nqb + q, h))
    kvspec = pl.BlockSpec((seq, HEAD_DIM), lambda b, h, q: (b, h))
    stat = pltpu.VMEM((rep * Q_BLOCK, HEAD_DIM), F32)
    return pl.pallas_call(
        functools.partial(_a_attn_prompt_kernel, rep=rep),
        grid=(batch, A_KV_HEADS, nqb),
        in_specs=[qspec, kvspec, kvspec,
                  pl.BlockSpec((None, n_chunks, Q_BLOCK, chunk), lambda b, h, q: (b * nqb + q, 0, 0, 0))],
        out_specs=qspec,
        out_shape=jax.ShapeDtypeStruct((batch * seq, A_Q_HEADS * HEAD_DIM), BF16),
        scratch_shapes=[stat, stat, stat],
        compiler_params=_cparams(("arbitrary", "arbitrary", "arbitrary")), name="a_attn_prompt",
    )(q, k, v, bias)


def _a_score_sample_kernel(pt_ref, iq_ref, iw_ref, *refs, n_pages):
    del pt_ref
    page_refs, ikn_ref, o_ref = refs[:-2], refs[-2], refs[-1]
    first = pl.program_id(1) * len(page_refs)
    ik = jnp.concatenate(
        [jnp.where(first + j == n_pages, ikn_ref[...], ref[...]) for j, ref in enumerate(page_refs)],
        axis=0).astype(BF16)
    iq = jnp.concatenate(
        [iq_ref[:, h * HEAD_DIM:(h + 1) * HEAD_DIM] for h in range(IDX_HEADS)], axis=0)
    rel = lax.dot_general(ik, iq.astype(BF16), (((1,), (1,)), ((), ())),
                          preferred_element_type=F32)
    w = jnp.maximum(rel, 0.0) * iw_ref[...]
    half = (IDX_HEADS * SUBLANES) // 2
    w = w[:, :half] + w[:, half:]
    shift = LANES // 2
    while shift >= SUBLANES:
        w = w + pltpu.roll(w, shift, 1)
        shift //= 2
    w = w * IDX_SCALE
    for j in range(len(page_refs)):
        o_ref[:, j * PAGE_SIZE:(j + 1) * PAGE_SIZE] = \
            w[j * PAGE_SIZE:(j + 1) * PAGE_SIZE].T[:SUBLANES, :]


def _a_score_sample(pt, iq_s, iw_row, cache_idx, ik_new, layer):
    nb, n_pages = pt.shape
    qrows = iq_s.shape[1]
    per_step = next(c for c in (5, 3, 1) if (n_pages + 1) % c == 0)

    def page_spec(j):
        return pl.BlockSpec(
            (None, None, PAGE_SIZE, HEAD_DIM),
            lambda b, p, pt_r: (layer, pt_r[b, jnp.minimum(p * per_step + j, n_pages - 1)], 0, 0))

    return pl.pallas_call(
        functools.partial(_a_score_sample_kernel, n_pages=n_pages),
        grid_spec=pltpu.PrefetchScalarGridSpec(
            num_scalar_prefetch=1, grid=(nb, (n_pages + 1) // per_step),
            in_specs=[pl.BlockSpec((None, qrows, IDX_HEADS * HEAD_DIM), lambda b, p, pt_r: (b, 0, 0)),
                      pl.BlockSpec((None, 1, IDX_HEADS * qrows), lambda b, p, pt_r: (b, 0, 0))] +
                     [page_spec(j) for j in range(per_step)] +
                     [pl.BlockSpec((None, PAGE_SIZE, HEAD_DIM), lambda b, p, pt_r: (b, 0, 0))],
            out_specs=pl.BlockSpec((None, qrows, per_step * PAGE_SIZE), lambda b, p, pt_r: (b, 0, p))),
        out_shape=jax.ShapeDtypeStruct((nb, qrows, (n_pages + 1) * PAGE_SIZE), F32),
        compiler_params=_cparams(("arbitrary", "arbitrary")), name="a_score_sample",
    )(pt, iq_s, iw_row, *([cache_idx] * per_step), ik_new)


def _a_select_sample_kernel(s_ref, o_ref, *, n_sel, pos0):
    q_pos = pos0 + lax.broadcasted_iota(I32, (s_ref.shape[0], 1), 0)
    o_ref[...] = _select_bias(s_ref[...], q_pos, n_sel)


def _a_select_sample(scores, *, n_sel, pos0):
    nb, qrows, nk = scores.shape
    spec = pl.BlockSpec((None, qrows, nk), lambda b: (b, 0, 0))
    return pl.pallas_call(
        functools.partial(_a_select_sample_kernel, n_sel=n_sel, pos0=pos0),
        grid=(nb,), in_specs=[spec], out_specs=spec,
        out_shape=jax.ShapeDtypeStruct(scores.shape, F32),
        compiler_params=_cparams(("arbitrary",)), name="a_select_sample",
    )(scores)


def _a_attn_sample_kernel(pt_ref, q_ref, bias_ref, kc_ref, vc_ref, kn_ref, vn_ref, o_ref,
                          m_ref, l_ref, acc_ref, *, n_pages, rep):
    del pt_ref
    p = pl.program_id(1)
    qrows = q_ref.shape[0]
    n_heads = A_KV_HEADS * rep
    n_cols = kc_ref.shape[0]

    @pl.when(p == 0)
    def _():
        m_ref[...] = jnp.full(m_ref.shape, -jnp.inf, F32)
        l_ref[...] = jnp.zeros(l_ref.shape, F32)
        acc_ref[...] = jnp.zeros(acc_ref.shape, F32)

    is_new = p == n_pages
    k = jnp.where(is_new, kn_ref[...], kc_ref[...]).astype(BF16)
    v = jnp.where(is_new, vn_ref[...], vc_ref[...]).astype(BF16)
    q = jnp.concatenate([q_ref[:, hd * HEAD_DIM:(hd + 1) * HEAD_DIM] for hd in range(n_heads)],
                        axis=0).astype(BF16)
    s = lax.dot_general(q, k, (((1,), (1,)), ((), ())), preferred_element_type=F32)
    kv_shift = A_KV_HEADS.bit_length() - 1
    row_shift = (rep * qrows).bit_length() - 1
    assert A_KV_HEADS == 1 << kv_shift and rep * qrows == 1 << row_shift
    spread = ((lax.broadcasted_iota(I32, (PAGE_SIZE, n_cols), 1) >> kv_shift) ==
              lax.broadcasted_iota(I32, (PAGE_SIZE, n_cols), 0))
    bias = jnp.dot(bias_ref[...].astype(BF16), jnp.where(spread, 1.0, 0.0).astype(BF16),
                   preferred_element_type=F32)
    s = s * ATTN_SCALE + jnp.concatenate([bias] * n_heads, axis=0)
    row_kv = lax.broadcasted_iota(I32, s.shape, 0) >> row_shift
    col_kv = lax.broadcasted_iota(I32, s.shape, 1) & (A_KV_HEADS - 1)
    s = jnp.where(row_kv == col_kv, s, MASK_BIAS)
    m_old = m_ref[...]
    m_new = jnp.maximum(m_old, jnp.max(s, axis=-1, keepdims=True))
    alpha = jnp.exp(m_old - m_new)
    pexp = jnp.concatenate([jnp.exp(s[:, j * LANES:(j + 1) * LANES] - m_new)
                            for j in range(n_cols // LANES)], axis=1)
    l_ref[...] = alpha * l_ref[...] + jnp.sum(pexp, axis=-1, keepdims=True)
    m_ref[...] = m_new
    acc_ref[...] = alpha * acc_ref[...] + jnp.dot(pexp.astype(BF16), v, preferred_element_type=F32)

    @pl.when(is_new)
    def _():
        o = acc_ref[...] * (1.0 / l_ref[...])
        for hd in range(n_heads):
            o_ref[:, hd * HEAD_DIM:(hd + 1) * HEAD_DIM] = o[hd * qrows:(hd + 1) * qrows]


def _a_attn_sample(pt, q_s, bias, cache_k, cache_v, k_new, v_new, layer):
    nb, n_pages = pt.shape
    qrows = q_s.shape[1]
    rep = A_Q_HEADS // A_KV_HEADS
    n_cols = PAGE_SIZE * A_KV_HEADS
    page_spec = pl.BlockSpec(
        (None, None, n_cols, HEAD_DIM),
        lambda b, p, pt_r: (layer, pt_r[b, jnp.minimum(p, n_pages - 1)], 0, 0))
    new_spec = pl.BlockSpec((None, n_cols, HEAD_DIM), lambda b, p, pt_r: (b, 0, 0))
    qspec = pl.BlockSpec((None, qrows, A_Q_HEADS * HEAD_DIM), lambda b, p, pt_r: (b, 0, 0))
    stat = pltpu.VMEM((A_Q_HEADS * qrows, HEAD_DIM), F32)
    return pl.pallas_call(
        functools.partial(_a_attn_sample_kernel, n_pages=n_pages, rep=rep),
        grid_spec=pltpu.PrefetchScalarGridSpec(
            num_scalar_prefetch=1, grid=(nb, n_pages + 1),
            in_specs=[qspec,
                      pl.BlockSpec((None, qrows, PAGE_SIZE), lambda b, p, pt_r: (b, 0, p)),
                      page_spec, page_spec, new_spec, new_spec],
            out_specs=qspec,
            scratch_shapes=[stat, stat, stat]),
        out_shape=jax.ShapeDtypeStruct(q_s.shape, F32),
        compiler_params=_cparams(("arbitrary", "arbitrary")), name="a_attn_sample",
    )(pt, q_s, bias, cache_k, cache_v, k_new, v_new)


def _dilated_bias(dist, window, dilation):
    z = dist if dilation == 1 else dist | lax.shift_left(dist & (dilation - 1), 16)
    return jnp.where(z >= 0, jnp.where(z <= window, 0.0, MASK_BIAS), MASK_BIAS)


def _dilated_merge(s_list, v_list, bias_list, rows, rep):
    m = None
    s_b = []
    for s, bias in zip(s_list, bias_list):
        sb = s * ATTN_SCALE + jnp.concatenate([bias] * rep, axis=0)
        s_b.append(sb)
        mg = jnp.max(sb, axis=-1, keepdims=True)
        m = mg if m is None else jnp.maximum(m, mg)
    l = jnp.zeros_like(m)
    o = None
    for sb, v in zip(s_b, v_list):
        p = jnp.exp(sb - m)
        l = l + jnp.sum(p, axis=-1, keepdims=True)
        pv = jnp.dot(p.astype(BF16), v, preferred_element_type=F32)
        o = pv if o is None else o + pv
    return o * (1.0 / l)


def _rows(start, size, stride):
    return pl.ds(start, size) if stride == 1 else pl.ds(start, size, stride=stride)


def _b_attn_prompt_kernel(q1_ref, q2_ref, q3_ref, k_ref, v_ref, o_ref,
                          qh_ref, kp_ref, vp_ref, og_ref, mg_ref, lg_ref, *, rep, dils):
    sb = pl.program_id(2)
    seq = k_ref.shape[0]
    sblk = q1_ref.shape[0]

    @pl.when(sb == 0)
    def _():
        for g, d in enumerate(dils):
            length = seq // d
            for r in range(d):
                kp_ref[g, r * length:(r + 1) * length, :] = k_ref[_rows(r, length, d), :].astype(BF16)
                vp_ref[g, r * length:(r + 1) * length, :] = v_ref[_rows(r, length, d), :].astype(BF16)

    for g, q_ref in enumerate((q1_ref, q2_ref, q3_ref)):
        for hd in range(rep):
            qh_ref[g, hd] = q_ref[:, hd * HEAD_DIM:(hd + 1) * HEAD_DIM]

    for g, d in enumerate(dils):
        length = seq // d
        per_step = sblk // d
        nj = min(B_TAPS, per_step)
        win = min(B_TAPS + nj, length)
        for r in range(d):
            for it in range(per_step // nj):
                loc = r + d * it * nj
                ja = sb * per_step + it * nj
                if win == length:
                    cstart = 0
                else:
                    cstart = pl.multiple_of(jnp.clip(ja - B_TAPS, 0, length - win), B_TAPS)
                qs = jnp.concatenate([qh_ref[g, hd, _rows(loc, nj, d), :] for hd in range(rep)],
                                     axis=0).astype(BF16)
                kw = kp_ref[g, pl.ds(r * length + cstart, win), :]
                vw = vp_ref[g, pl.ds(r * length + cstart, win), :]
                s = lax.dot_general(qs, kw, (((1,), (1,)), ((), ())), preferred_element_type=F32)
                dist = (ja + lax.broadcasted_iota(I32, (nj, win), 0)) - \
                       (cstart + lax.broadcasted_iota(I32, (nj, win), 1))
                bias = _dilated_bias(dist, B_TAPS, 1)
                s = s * ATTN_SCALE + jnp.concatenate([bias] * rep, axis=0)
                m = jnp.max(s, axis=-1, keepdims=True)
                p = jnp.exp(s - m)
                l = jnp.sum(p, axis=-1, keepdims=True)
                o = jnp.dot(p.astype(BF16), vw, preferred_element_type=F32)
                for hd in range(rep):
                    rows = _rows(loc, nj, d)
                    og_ref[g, hd, rows, :] = o[hd * nj:(hd + 1) * nj]
                    mg_ref[g, hd, rows, :] = jnp.broadcast_to(m[hd * nj:(hd + 1) * nj], (nj, HEAD_DIM))
                    lg_ref[g, hd, rows, :] = jnp.broadcast_to(l[hd * nj:(hd + 1) * nj], (nj, HEAD_DIM))

    n_groups = len(dils)
    for hd in range(rep):
        m = mg_ref[0, hd]
        for g in range(1, n_groups):
            m = jnp.maximum(m, mg_ref[g, hd])
        num = jnp.zeros((sblk, HEAD_DIM), F32)
        den = jnp.zeros((sblk, HEAD_DIM), F32)
        for g in range(n_groups):
            w = jnp.exp(mg_ref[g, hd] - m)
            num = num + w * og_ref[g, hd]
            den = den + w * lg_ref[g, hd]
        o_ref[:, hd * HEAD_DIM:(hd + 1) * HEAD_DIM] = (num * (1.0 / den)).astype(o_ref.dtype)


def _b_attn_prompt(q3, k, v, *, batch, seq):
    dils = tuple(d for _, d in DILATED_GROUPS)
    assert all(w == d * B_TAPS for w, d in DILATED_GROUPS) and seq % B_QUERY_BLOCK == 0
    assert all(B_QUERY_BLOCK % d == 0 and (seq // d) % B_TAPS == 0 for d in dils)
    nsb = seq // B_QUERY_BLOCK
    rep = B_Q_HEADS // B_KV_HEADS
    w = rep * HEAD_DIM
    n_groups = len(dils)

    def qspec(g):
        return pl.BlockSpec((B_QUERY_BLOCK, w), lambda b, h, s: (b * nsb + s, g * B_KV_HEADS + h))

    kvspec = pl.BlockSpec((seq, HEAD_DIM), lambda b, h, s: (b, h))
    part = pltpu.VMEM((n_groups, rep, B_QUERY_BLOCK, HEAD_DIM), F32)
    perm = pltpu.VMEM((n_groups, seq, HEAD_DIM), BF16)
    return pl.pallas_call(
        functools.partial(_b_attn_prompt_kernel, rep=rep, dils=dils),
        grid=(batch, B_KV_HEADS, nsb),
        in_specs=[qspec(0), qspec(1), qspec(2), kvspec, kvspec],
        out_specs=pl.BlockSpec((B_QUERY_BLOCK, w), lambda b, h, s: (b * nsb + s, h)),
        out_shape=jax.ShapeDtypeStruct((batch * seq, B_Q_HEADS * HEAD_DIM), BF16),
        scratch_shapes=[part, perm, perm, part, part, part],
        compiler_params=_cparams(("arbitrary", "arbitrary", "arbitrary")), name="b_attn_prompt",
    )(q3, q3, q3, k, v)


def _b_attn_sample_kernel(q1_ref, q2_ref, q3_ref, k_ref, v_ref, o_ref, *, rep, groups, q_idx0):
    rows = q1_ref.shape[0]
    n_ctx = k_ref.shape[0]
    k = k_ref[...].astype(BF16)
    v = v_ref[...].astype(BF16)
    dist = (q_idx0 + lax.broadcasted_iota(I32, (rows, n_ctx), 0)) - \
        lax.broadcasted_iota(I32, (rows, n_ctx), 1)
    s_list, bias_list = [], []
    for q_ref, (window, dilation) in zip((q1_ref, q2_ref, q3_ref), groups):
        s_list.append(lax.dot_general(_stack_heads(q_ref, rep), k, (((1,), (1,)), ((), ())),
                                      preferred_element_type=F32))
        bias_list.append(_dilated_bias(dist, window, dilation))
    o = _dilated_merge(s_list, [v] * len(groups), bias_list, rows, rep)
    for r in range(rep):
        o_ref[:, r * HEAD_DIM:(r + 1) * HEAD_DIM] = o[r * rows:(r + 1) * rows]


def _b_attn_sample(q3_s, ctx_k, ctx_v, *, q_idx0):
    nb, qrows, _ = q3_s.shape
    n_ctx = ctx_k.shape[1]
    rep = B_Q_HEADS // B_KV_HEADS
    w = rep * HEAD_DIM

    def qspec(g):
        return pl.BlockSpec((None, qrows, w), lambda b, h: (b, 0, g * B_KV_HEADS + h))

    kvspec = pl.BlockSpec((None, n_ctx, HEAD_DIM), lambda b, h: (b, 0, h))
    return pl.pallas_call(
        functools.partial(_b_attn_sample_kernel, rep=rep, groups=DILATED_GROUPS, q_idx0=q_idx0),
        grid=(nb, B_KV_HEADS),
        in_specs=[qspec(0), qspec(1), qspec(2), kvspec, kvspec],
        out_specs=pl.BlockSpec((None, qrows, w), lambda b, h: (b, 0, h)),
        out_shape=jax.ShapeDtypeStruct((nb, qrows, B_Q_HEADS * HEAD_DIM), F32),
        compiler_params=_cparams(("arbitrary", "arbitrary")), name="b_attn_sample",
    )(q3_s, q3_s, q3_s, ctx_k, ctx_v)


def _router_kernel(x_ref, w_ref, o_ref):
    logits = jnp.dot(x_ref[...], w_ref[...], preferred_element_type=F32,
                     precision=lax.Precision.HIGHEST)
    lane = lax.broadcasted_iota(I32, logits.shape, 1).astype(F32)
    lg = jnp.where(lane < N_EXPERTS, logits, -jnp.inf)
    m1 = jnp.max(lg, axis=-1, keepdims=True)
    i1 = jnp.min(jnp.where(lg == m1, lane, float(LANES)), axis=-1, keepdims=True)
    lg2 = jnp.where(lane == i1, -jnp.inf, lg)
    m2 = jnp.max(lg2, axis=-1, keepdims=True)
    i2 = jnp.min(jnp.where(lg2 == m2, lane, float(LANES)), axis=-1, keepdims=True)
    e = jnp.exp(m2 - m1)
    g1 = 1.0 / (1.0 + e)
    g2 = e * g1
    o_ref[...] = jnp.where(lane == 0, g1, jnp.where(lane == 1, g2, jnp.where(
        lane == 2, i1, jnp.where(lane == 3, i2, 0.0))))


def _router(h, w_router_pad, layer):
    m, d = h.shape
    return pl.pallas_call(
        _router_kernel, grid=(m // TM_ROW,),
        in_specs=[pl.BlockSpec((TM_ROW, d), lambda i: (i, 0)),
                  pl.BlockSpec((None, d, LANES), lambda i: (layer, 0, 0))],
        out_specs=pl.BlockSpec((TM_ROW, LANES), lambda i: (i, 0)),
        out_shape=jax.ShapeDtypeStruct((m, LANES), F32),
        compiler_params=_cparams(("arbitrary",)), name="router",
    )(h, w_router_pad)


def _row_copy(src_hbm, row, dst, dst_row, sem):
    return pltpu.make_async_copy(src_hbm.at[pl.ds(row, 1)], dst.at[pl.ds(dst_row, 1)], sem)


def _dispatch_kernel(live_ref, tok_ref, h_hbm, o_ref, buf, sem):
    i = pl.program_id(0)
    rows = buf.shape[0]

    @pl.when(live_ref[i] > 0)
    def _():
        def start(r, c):
            _row_copy(h_hbm, tok_ref[0, r], buf, r, sem).start()
            return c

        def wait(r, c):
            _row_copy(h_hbm, 0, buf, r, sem).wait()
            return c

        lax.fori_loop(0, rows, start, 0)
        lax.fori_loop(0, rows, wait, 0)
        o_ref[...] = buf[...].astype(o_ref.dtype)

    @pl.when(live_ref[i] == 0)
    def _():
        o_ref[...] = jnp.zeros(o_ref.shape, o_ref.dtype)


def _dispatch(h, tok_of, tile_live):
    d = h.shape[1]
    n_rows = tok_of.shape[0]
    return pl.pallas_call(
        _dispatch_kernel,
        grid_spec=pltpu.PrefetchScalarGridSpec(
            num_scalar_prefetch=1, grid=(n_rows // TM_ROW,),
            in_specs=[pl.BlockSpec((None, 1, TM_ROW), lambda i, live: (i, 0, 0),
                                   memory_space=pltpu.SMEM),
                      pl.BlockSpec(memory_space=pl.ANY)],
            out_specs=pl.BlockSpec((TM_ROW, d), lambda i, live: (i, 0)),
            scratch_shapes=[pltpu.VMEM((TM_ROW, d), F32), pltpu.SemaphoreType.DMA(())]),
        out_shape=jax.ShapeDtypeStruct((n_rows, d), BF16),
        compiler_params=_cparams(("arbitrary",)), name="moe_dispatch",
    )(tile_live, tok_of.reshape(n_rows // TM_ROW, 1, TM_ROW), h)


def _combine_kernel(pos_ref, y_hbm, h_ref, gate_ref, g_ref, b_ref, o_ref, ob_ref, buf, sem):
    rows = h_ref.shape[0]

    def start(r, c):
        for kk in range(2):
            _row_copy(y_hbm, pos_ref[0, 2 * r + kk], buf.at[kk], r, sem).start()
        return c

    def wait(r, c):
        for kk in range(2):
            _row_copy(y_hbm, 0, buf.at[kk], r, sem).wait()
        return c

    lax.fori_loop(0, rows, start, 0)
    lax.fori_loop(0, rows, wait, 0)
    gates = gate_ref[...]
    ffn = gates[:, 0:1] * buf[0] + gates[:, 1:2] * buf[1]
    o = _layer_norm_rows(DEEPNORM_ALPHA * h_ref[...] + ffn, g_ref[...], b_ref[...])
    o_ref[...] = o
    ob_ref[...] = o.astype(BF16)


def _combine_deepnorm(h, y_sorted, pos, gates, g, b):
    m, d = h.shape
    row = pl.BlockSpec((TM_ROW, d), lambda i: (i, 0))
    vec = pl.BlockSpec((1, d), lambda i: (0, 0))
    return pl.pallas_call(
        _combine_kernel,
        grid=(m // TM_ROW,),
        in_specs=[pl.BlockSpec((None, 1, 2 * TM_ROW), lambda i: (i, 0, 0), memory_space=pltpu.SMEM),
                  pl.BlockSpec(memory_space=pl.ANY), row,
                  pl.BlockSpec((TM_ROW, LANES), lambda i: (i, 0)), vec, vec],
        out_specs=[row, row],
        scratch_shapes=[pltpu.VMEM((2, TM_ROW, d), F32), pltpu.SemaphoreType.DMA(())],
        out_shape=[jax.ShapeDtypeStruct((m, d), F32), jax.ShapeDtypeStruct((m, d), BF16)],
        compiler_params=_cparams(("arbitrary",)), name="moe_combine",
    )(pos.reshape(m // TM_ROW, 1, 2 * TM_ROW), y_sorted, h, gates, g.reshape(1, d), b.reshape(1, d))


def _moe_plan(top_idx, n_tiles_max):
    n_assign = top_idx.size
    e_flat = top_idx.reshape(-1)
    onehot = (e_flat[:, None] == jnp.arange(N_EXPERTS, dtype=I32)[None, :]).astype(I32)
    csum = jnp.cumsum(onehot, axis=0)
    counts = csum[-1]
    rank = jnp.sum((csum - onehot) * onehot, axis=1)
    tiles_e = (counts + TM_MOE - 1) // TM_MOE
    tile_end = jnp.cumsum(tiles_e)
    tile_start = tile_end - tiles_e
    pos = jnp.sum(onehot * tile_start[None, :], axis=1) * TM_MOE + rank
    n_rows = n_tiles_max * TM_MOE
    tok_of = jnp.zeros((n_rows,), I32).at[pos].set(jnp.arange(n_assign, dtype=I32) // 2)
    n_used = tile_end[-1]
    t = jnp.arange(n_tiles_max, dtype=I32)
    gid = jnp.minimum(jnp.sum((t[:, None] >= tile_end[None, :]).astype(I32), axis=1), N_EXPERTS - 1)
    nv = jnp.clip(counts[gid] - (t - tile_start[gid]) * TM_MOE, 0, TM_MOE)
    nv = jnp.where(t < n_used, nv, 0).astype(I32)
    gid = jnp.where(t < n_used, gid, gid[jnp.maximum(n_used - 1, 0)])
    sub = jnp.arange(n_rows // TM_ROW, dtype=I32)
    live = (nv[sub * TM_ROW // TM_MOE] > (sub * TM_ROW) % TM_MOE).astype(I32)
    return pos, tok_of, gid, nv, n_used.astype(I32), live


def _rope_tables(pos):
    half = HEAD_DIM // 2
    inv_freq = jnp.power(ROPE_THETA, -jnp.arange(half, dtype=F32) / half)
    ang = pos.astype(F32)[:, None] * inv_freq[None, :]
    cos, sin = jnp.cos(ang), jnp.sin(ang)
    return jnp.concatenate([cos, cos], axis=1), jnp.concatenate([-sin, sin], axis=1)


def _sample_rows(a, n_prompt, nb, nq, qrows):
    x = a[n_prompt:n_prompt + nb * nq].reshape(nb, nq, a.shape[1]).astype(F32)
    return jnp.pad(x, ((0, 0), (0, qrows - nq), (0, 0)))


def _as_page(rows_s, nb, nq):
    x = rows_s.reshape(nb, nq, rows_s.shape[1])
    return jnp.pad(x, ((0, 0), (0, PAGE_SIZE - nq), (0, 0)))


def kernel(x_prompt, x_sample, cache_a_k, cache_a_v, cache_a_idx, state_b_k, state_b_v, page_table,
           w_a_in, w_a_out, w_kv_shared, w_b_q, w_b_out, ln_mix_g, ln_mix_b, ln_ffn_g, ln_ffn_b,
           w_ffn_in, w_ffn_down, w_router, w_exp_in, w_exp_down):
    batch, seq, d = x_prompt.shape
    nb, nq, _ = x_sample.shape
    n_prompt, n_samp = batch * seq, nb * nq
    past_len = page_table.shape[1] * PAGE_SIZE
    n_a_layers = w_a_in.shape[0]
    m_pad = pl.cdiv(n_prompt + n_samp, TM_DENSE) * TM_DENSE
    assert m_pad % TM_ROW == 0 and n_prompt % TM_ROW == 0
    qrows = SUBLANES
    n_sel_p = min(IDX_TOPK_MAX, seq // 4)
    n_sel_s = min(IDX_TOPK_MAX, (past_len + nq) // 4)
    w_buf = state_b_k.shape[1]

    pos_all = jnp.concatenate([jnp.tile(jnp.arange(seq, dtype=I32), batch),
                               jnp.tile(past_len + jnp.arange(nq, dtype=I32), nb),
                               jnp.zeros((m_pad - n_prompt - n_samp,), I32)])
    tabs = _rope_tables(pos_all)

    h = jnp.concatenate([x_prompt.reshape(n_prompt, d), x_sample.reshape(n_samp, d),
                         jnp.zeros((m_pad - n_prompt - n_samp, d), F32)], axis=0)
    hb = h.astype(BF16)

    qw = A_Q_HEADS * HEAD_DIM
    kvw = A_KV_HEADS * HEAD_DIM
    iqw = IDX_HEADS * HEAD_DIM
    cache_k_rows = cache_a_k.reshape(cache_a_k.shape[:2] + (-1, HEAD_DIM))
    cache_v_rows = cache_a_v.reshape(cache_a_v.shape[:2] + (-1, HEAD_DIM))
    w_router_pad = jnp.pad(w_router, ((0, 0), (0, 0), (0, LANES - N_EXPERTS)))
    w_exp_in2 = w_exp_in.reshape((-1,) + w_exp_in.shape[2:])
    w_exp_down2 = w_exp_down.reshape((-1,) + w_exp_down.shape[2:])
    w_kv3 = w_kv_shared[None]
    w_iw = w_a_in[:, :, qw + 2 * kvw + iqw + HEAD_DIM:]
    n_tiles_moe = (2 * m_pad) // TM_MOE + N_EXPERTS

    def pad_rows(mix_p, mix_s):
        return jnp.concatenate([mix_p, mix_s[:, :nq].reshape(n_samp, -1).astype(BF16),
                                jnp.zeros((m_pad - n_prompt - n_samp, mix_p.shape[1]), BF16)], axis=0)

    def heads(a, lead):
        return a.reshape(lead + (A_KV_HEADS, HEAD_DIM))

    ak, av, ai = [], [], []
    kb = vb = ctx_k = ctx_v = kb_new = vb_new = None
    for l in range(DEPTH):
        if l < n_a_layers:
            proj = functools.partial(_dense, hb, w_a_in, l, tm=TM_DENSE)
            q = proj(col0=0, n_out=qw, tn=TN_PROJ, mode="rope", tabs=tabs, name="a_q")
            k = proj(col0=qw, n_out=kvw, tn=TN_PROJ, mode="rope", tabs=tabs, name="a_k")
            v = proj(col0=qw + kvw, n_out=kvw, tn=TN_PROJ, name="a_v")
            iq = proj(col0=qw + 2 * kvw, n_out=iqw, tn=TN_PROJ, mode="rope", tabs=tabs, name="a_iq")
            ik = proj(col0=qw + 2 * kvw + iqw, n_out=HEAD_DIM, tn=LANES, mode="rope", tabs=tabs,
                      name="a_ik")
            iw = _dense(hb, w_iw, l, col0=0, n_out=IDX_HEADS, tm=TM_DENSE, tn=IDX_HEADS, name="a_iw")
            ak.append(k)
            av.append(v)
            ai.append(ik)
            bias_p = _a_select_prompt(iq, ik, iw.T, batch=batch, seq=seq, n_sel=n_sel_p)
            mix_p = _a_attn_prompt(q, k, v, bias_p, batch=batch, seq=seq)
            iq_s = _sample_rows(iq, n_prompt, nb, nq, qrows)
            iw_s = _sample_rows(iw, n_prompt, nb, nq, qrows)
            iw_row = jnp.transpose(iw_s, (0, 2, 1)).reshape(nb, 1, IDX_HEADS * qrows)
            scores = _a_score_sample(page_table, iq_s, iw_row, cache_a_idx,
                                     _as_page(ik[n_prompt:n_prompt + n_samp], nb, nq), l)
            bias_s = _a_select_sample(scores, n_sel=n_sel_s, pos0=past_len)
            mix_s = _a_attn_sample(page_table, _sample_rows(q, n_prompt, nb, nq, qrows), bias_s,
                                   cache_k_rows, cache_v_rows,
                                   _as_page(k[n_prompt:n_prompt + n_samp], nb, nq).reshape(nb, -1, HEAD_DIM),
                                   _as_page(v[n_prompt:n_prompt + n_samp], nb, nq).reshape(nb, -1, HEAD_DIM), l)
            w_out, lo = w_a_out, l
        else:
            j = l - n_a_layers
            q3 = _dense(hb, w_b_q, j, col0=0, n_out=w_b_q.shape[2], tm=TM_DENSE, tn=TN_PROJ,
                        mode="rope", tabs=tabs, name="b_q")
            mix_p = _b_attn_prompt(q3, kb, vb, batch=batch, seq=seq)
            mix_s = _b_attn_sample(_sample_rows(q3, n_prompt, nb, nq, qrows), ctx_k, ctx_v,
                                   q_idx0=w_buf)
            w_out, lo = w_b_out, j
        sub = _dense(pad_rows(mix_p, mix_s), w_out, lo, col0=0, n_out=d, tm=TM_DENSE, tn=TN_PROJ,
                     name="mix_out")
        h, hb = _deepnorm(h, sub, ln_mix_g[l], ln_mix_b[l])

        f = l // 2
        if l % 2 == 0:
            act = _dense(hb, w_ffn_in, f, col0=0, n_out=D_FF, up_col0=D_FF, tm=TM_DENSE, tn=TN_FFN,
                         mode="swiglu", out_dtype=BF16, name="ffn_in")
            sub = _dense(act, w_ffn_down, f, col0=0, n_out=d, tm=TM_DENSE, tn=TN_DOWN, tk=TK_DOWN,
                         name="ffn_down")
            h, hb = _deepnorm(h, sub, ln_ffn_g[l], ln_ffn_b[l])
        else:
            route = _router(h, w_router_pad, f)
            gates = route
            top_idx = route[:, 2:4].astype(I32)
            pos, tok_of, gid, nv, n_used, live = _moe_plan(top_idx, n_tiles_moe)
            gid = gid + f * N_EXPERTS
            xs = _dispatch(h, tok_of, live)
            act = _matmul(xs, w_exp_in2, gid, nv, n_used, col0=0, n_out=D_FF, up_col0=D_FF,
                          tm=TM_MOE, tn=TN_FFN, mode="swiglu", out_dtype=BF16, n_paths=MOE_PATHS,
                          name="moe_in")
            ys = _matmul(act, w_exp_down2, gid, nv, n_used, col0=0, n_out=d, tm=TM_MOE, tn=TN_DOWN,
                         tk=TK_DOWN, n_paths=MOE_PATHS, name="moe_down")
            h, hb = _combine_deepnorm(h, ys, pos, gates, ln_ffn_g[l], ln_ffn_b[l])

        if l == n_a_layers - 1:
            kb = _dense(hb, w_kv3, 0, col0=0, n_out=kvw, tm=TM_DENSE, tn=TN_PROJ, mode="rope",
                        tabs=tabs, name="kv_k")
            vb = _dense(hb, w_kv3, 0, col0=kvw, n_out=kvw, tm=TM_DENSE, tn=TN_PROJ, name="kv_v")
            kb_new = kb[n_prompt:n_prompt + n_samp].reshape(nb, nq, kvw)
            vb_new = vb[n_prompt:n_prompt + n_samp].reshape(nb, nq, kvw)
            ctx_pad = (-(w_buf + nq)) % LANES
            ctx_k = jnp.concatenate([state_b_k.reshape(nb, w_buf, kvw), kb_new,
                                     jnp.zeros((nb, ctx_pad, kvw), F32)], axis=1)
            ctx_v = jnp.concatenate([state_b_v.reshape(nb, w_buf, kvw), vb_new,
                                     jnp.zeros((nb, ctx_pad, kvw), F32)], axis=1)

    y_prompt = h[:n_prompt].reshape(batch, seq, d)
    y_sample = h[n_prompt:n_prompt + n_samp].reshape(nb, nq, d)
    new_a_k_prompt = jnp.stack([heads(a[:n_prompt], (batch, seq)) for a in ak])
    new_a_v_prompt = jnp.stack([heads(a[:n_prompt], (batch, seq)) for a in av])
    new_a_idx_prompt = jnp.stack([a[:n_prompt].reshape(batch, seq, HEAD_DIM) for a in ai])
    new_a_k_sample = jnp.stack([heads(a[n_prompt:n_prompt + n_samp], (nb, nq)) for a in ak])
    new_a_v_sample = jnp.stack([heads(a[n_prompt:n_prompt + n_samp], (nb, nq)) for a in av])
    new_a_idx_sample = jnp.stack([a[n_prompt:n_prompt + n_samp].reshape(nb, nq, HEAD_DIM) for a in ai])
    w_keep = min(2048, seq)
    new_b_k_prompt = heads(kb[:n_prompt], (batch, seq))[:, seq - w_keep:]
    new_b_v_prompt = heads(vb[:n_prompt], (batch, seq))[:, seq - w_keep:]
    new_b_k_sample = jnp.concatenate([state_b_k[:, nq:], heads(kb_new, (nb, nq))], axis=1)
    new_b_v_sample = jnp.concatenate([state_b_v[:, nq:], heads(vb_new, (nb, nq))], axis=1)
    return (y_prompt, y_sample, new_a_k_prompt, new_a_v_prompt, new_a_idx_prompt, new_a_k_sample,
            new_a_v_sample, new_a_idx_sample, new_b_k_prompt, new_b_v_prompt, new_b_k_sample,
            new_b_v_sample)
```
